```python
import jax
import jax.numpy as jnp
from jax import lax
import numpy as np

D_MODEL = 1024
BATCH = 4
SEQ = 8192
DEPTH = 2
DEC_BATCH = 16
DEC_SEQ = 16
PAST_LEN = 1024

CHUNK = 64
Q_BLOCK = 128
ROPE_THETA = 10000.0
NORM_EPS = 1e-6
NEG_INF = -1e30

LRU_WIDTH = 1024
LRU_BLOCKS = 8
LRU_BLOCK = LRU_WIDTH // LRU_BLOCKS
CONV_W = 4
LRU_C = 8.0

MLA_HEADS = 8
MLA_Q_LORA = 384
MLA_KV_LORA = 256
MLA_NOPE = 128
MLA_ROPE = 64
MLA_V = 128

DSA_HEADS = 8
DSA_KV_HEADS = 4
DSA_GROUP = DSA_HEADS // DSA_KV_HEADS
DSA_HD = 128
IDX_HEADS = 16
IDX_HD = 64
IDX_ROPE = 32
TOPK_MAX = 256

N_BRANCH = 3
D_FF = ((8 * D_MODEL + 3 * 256 - 1) // (3 * 256)) * 256

IN_SPLITS = (
    LRU_WIDTH,
    LRU_WIDTH,
    MLA_Q_LORA,
    MLA_KV_LORA,
    MLA_ROPE,
    DSA_HEADS * DSA_HD,
    DSA_KV_HEADS * DSA_HD,
    DSA_KV_HEADS * DSA_HD,
    IDX_HEADS * IDX_HD,
    IDX_HD,
    IDX_HEADS,
    N_BRANCH * D_MODEL,
)
N_IN = sum(IN_SPLITS)

kernel_name = 'hybrid_lru_mla_dsa_stream_step'


def _rmsnorm(x, g):
    x32 = x.astype(jnp.float32)
    y = x32 * lax.rsqrt(jnp.mean(x32 * x32, axis=-1, keepdims=True) + NORM_EPS)
    return (y * g.astype(jnp.float32)).astype(x.dtype)


def _layernorm(x, g, b):
    x32 = x.astype(jnp.float32)
    xc = x32 - jnp.mean(x32, axis=-1, keepdims=True)
    y = xc * lax.rsqrt(jnp.mean(xc * xc, axis=-1, keepdims=True) + NORM_EPS)
    return (y * g.astype(jnp.float32) + b.astype(jnp.float32)).astype(x.dtype)


def _rope(x, pos):
    d = x.shape[-1]
    inv = ROPE_THETA ** (-jnp.arange(0, d, 2, dtype=jnp.float32) / d)
    ang = pos.astype(jnp.float32)[:, None] * inv[None, :]
    cos = jnp.cos(ang)[None, :, None, :]
    sin = jnp.sin(ang)[None, :, None, :]
    x32 = x.astype(jnp.float32)
    x1, x2 = x32[..., : d // 2], x32[..., d // 2:]
    return jnp.concatenate([x1 * cos - x2 * sin, x2 * cos + x1 * sin], axis=-1).astype(x.dtype)


def _partial_rope(x, pos):
    return jnp.concatenate([_rope(x[..., :IDX_ROPE], pos), x[..., IDX_ROPE:]], axis=-1)


def _chunk_visible(q_pos, k_pos):
    return (k_pos[None, :] // CHUNK) <= (q_pos[:, None] // CHUNK)


def _split_cols(z):
    parts, off = [], 0
    for w in IN_SPLITS:
        parts.append(z[..., off:off + w])
        off += w
    return parts


def _over_query_blocks(fn, q_arrays, q_pos):
    T = q_pos.shape[0]
    if T <= Q_BLOCK or T % Q_BLOCK != 0:
        return fn(*q_arrays, q_pos)
    nb = T // Q_BLOCK
    blocks = tuple(jnp.moveaxis(a.reshape(a.shape[0], nb, Q_BLOCK, *a.shape[2:]), 1, 0) for a in q_arrays)
    out = lax.map(lambda args: fn(*args[0], args[1]), (blocks, q_pos.reshape(nb, Q_BLOCK)))
    out = jnp.moveaxis(out, 0, 1)
    return out.reshape(out.shape[0], T, *out.shape[3:])


def _lin_rec_combine(left, right):
    a1, b1 = left
    a2, b2 = right
    return a1 * a2, a2 * b1 + b2


def _rglru(gate_in, x_in, conv_buf, h0, conv_w, conv_b, wa, ba, wx, bx, lam):
    B, T, W = x_in.shape
    xc = jnp.concatenate([conv_buf.astype(x_in.dtype), x_in], axis=1)
    u = conv_b + xc[:, 0:T] * conv_w[0]
    for j in range(1, CONV_W):
        u = u + xc[:, j:j + T] * conv_w[j]
    ub = u.reshape(B, T, LRU_BLOCKS, LRU_BLOCK)
    r = jax.nn.sigmoid(jnp.einsum('btnc,ncd->btnd', ub, wa) + ba).reshape(B, T, W).astype(jnp.float32)
    i = jax.nn.sigmoid(jnp.einsum('btnc,ncd->btnd', ub, wx) + bx).reshape(B, T, W).astype(jnp.float32)
    log_a = LRU_C * r * jax.nn.log_sigmoid(lam.astype(jnp.float32))
    a = jnp.exp(log_a)
    b = jnp.sqrt(-jnp.expm1(2.0 * log_a)) * (i * u.astype(jnp.float32))
    b = b.at[:, 0].add(a[:, 0] * h0.astype(jnp.float32))
    _, h = lax.associative_scan(_lin_rec_combine, (a, b), axis=1)
    out = (h * jax.nn.gelu(gate_in.astype(jnp.float32))).astype(x_in.dtype)
    return out, xc[:, T:], h[:, -1]


def _mla(q_lat, kv_lat, kr_raw, q_pos, past_lat, past_kr, q_norm_g, kv_norm_g, w_q_up, w_kv_up):
    B, T, _ = q_lat.shape
    q = jnp.einsum('btr,rhd->bthd', _rmsnorm(q_lat, q_norm_g), w_q_up)
    q = jnp.concatenate([q[..., :MLA_NOPE], _rope(q[..., MLA_NOPE:], q_pos)], axis=-1)
    lat_new = _rmsnorm(kv_lat, kv_norm_g)
    kr_new = _rope(kr_raw[:, :, None, :], q_pos)[:, :, 0, :]
    lat_all = lat_new if past_lat is None else jnp.concatenate([past_lat.astype(lat_new.dtype), lat_new], axis=1)
    kr_all = kr_new if past_kr is None else jnp.concatenate([past_kr.astype(kr_new.dtype), kr_new], axis=1)
    S = lat_all.shape[1]
    kv = jnp.einsum('bsr,rhd->bshd', lat_all, w_kv_up)
    k = jnp.concatenate([kv[..., :MLA_NOPE], jnp.broadcast_to(kr_all[:, :, None, :], (B, S, MLA_HEADS, MLA_ROPE))], axis=-1)
    v = kv[..., MLA_NOPE:]
    k_pos = jnp.arange(S, dtype=jnp.int32)
    scale = (MLA_NOPE + MLA_ROPE) ** -0.5

    def block(qb, qpb):
        s = jnp.einsum('bqhd,bshd->bhqs', qb, k).astype(jnp.float32) * scale
        s = jnp.where(_chunk_visible(qpb, k_pos)[None, None], s, NEG_INF)
        p = jax.nn.softmax(s, axis=-1).astype(v.dtype)
        return jnp.einsum('bhqs,bshd->bqhd', p, v)

    o = _over_query_blocks(block, (q,), q_pos)
    return o.reshape(B, T, MLA_HEADS * MLA_V), lat_new, kr_new


def _dsa(q_raw, k_raw, v_raw, iq_raw, ik_raw, iw_raw, q_pos, past_k, past_v, past_ki, ik_g, ik_b):
    B, T, _ = q_raw.shape
    q = _rope(q_raw.reshape(B, T, DSA_HEADS, DSA_HD), q_pos)
    k_new = _rope(k_raw.reshape(B, T, DSA_KV_HEADS, DSA_HD), q_pos)
    v_new = v_raw.reshape(B, T, DSA_KV_HEADS, DSA_HD)
    iq = _partial_rope(iq_raw.reshape(B, T, IDX_HEADS, IDX_HD), q_pos)
    ki_new = _partial_rope(_layernorm(ik_raw, ik_g, ik_b)[:, :, None, :], q_pos)[:, :, 0, :]
    iw = iw_raw * (IDX_HEADS * IDX_HD) ** -0.5
    if past_k is None:
        k_all, v_all, ki_all = k_new, v_new, ki_new
    else:
        k_all = jnp.concatenate([past_k.astype(k_new.dtype), k_new], axis=1)
        v_all = jnp.concatenate([past_v.astype(v_new.dtype), v_new], axis=1)
        ki_all = jnp.concatenate([past_ki.astype(ki_new.dtype), ki_new], axis=1)
    S = k_all.shape[1]
    topk = min(TOPK_MAX, S // 4)
    k_pos = jnp.arange(S, dtype=jnp.int32)
    bidx = jnp.arange(B)[:, None, None]
    scale = DSA_HD ** -0.5

    def block(qb, iqb, iwb, qpb):
        Q = qb.shape[1]
        logits = jnp.einsum('bqhd,bsd->bqhs', iqb, ki_all).astype(jnp.float32)
        score = jnp.einsum('bqh,bqhs->bqs', iwb.astype(jnp.float32), jax.nn.relu(logits))
        score = jnp.where(_chunk_visible(qpb, k_pos)[None], score, -jnp.inf)
        _, sel = lax.top_k(score, topk)
        valid = (sel // CHUNK) <= (qpb[None, :, None] // CHUNK)
        kg = k_all[bidx, sel]
        vg = v_all[bidx, sel]
        qg = qb.reshape(B, Q, DSA_KV_HEADS, DSA_GROUP, DSA_HD)
        s = jnp.einsum('bqkgd,bqnkd->bqkgn', qg, kg).astype(jnp.float32) * scale
        s = jnp.where(valid[:, :, None, None, :], s, NEG_INF)
        p = jax.nn.softmax(s, axis=-1).astype(vg.dtype)
        o = jnp.einsum('bqkgn,bqnkd->bqkgd', p, vg)
        return o.reshape(B, Q, DSA_HEADS * DSA_HD)

    o = _over_query_blocks(block, (q, iq, iw), q_pos)
    return o, k_new, v_new, ki_new


def _layer(x, pos, past, w):
    (attn_norm_g, w_in, b_gates, lru_conv_w, lru_conv_b, lru_wa, lru_ba, lru_wx, lru_bx, lru_lambda,
     mla_q_norm_g, mla_kv_norm_g, mla_w_q_up, mla_w_kv_up, idx_k_norm_g, idx_k_norm_b,
     w_branch_a, w_branch_b, w_branch_c, w_out, ffn_norm_g, w_ffn_gate, w_ffn_up, w_ffn_down) = w
    (past_lat, past_kr, past_k, past_v, past_ki, conv_buf, h0) = past
    B, T, _ = x.shape
    xn = _rmsnorm(x, attn_norm_g)
    z = jnp.einsum('btd,dn->btn', xn, w_in)
    (a_gate, a_x, b_q, b_kv, b_kr, c_q, c_k, c_v, c_iq, c_ik, c_iw, g_raw) = _split_cols(z)
    o_a, conv_new, h_new = _rglru(a_gate, a_x, conv_buf, h0, lru_conv_w, lru_conv_b, lru_wa, lru_ba, lru_wx, lru_bx, lru_lambda)
    o_b, lat_new, kr_new = _mla(b_q, b_kv, b_kr, pos, past_lat, past_kr, mla_q_norm_g, mla_kv_norm_g, mla_w_q_up, mla_w_kv_up)
    o_c, k_new, v_new, ki_new = _dsa(c_q, c_k, c_v, c_iq, c_ik, c_iw, pos, past_k, past_v, past_ki, idx_k_norm_g, idx_k_norm_b)
    g = jax.nn.sigmoid((g_raw + b_gates).astype(jnp.float32)).astype(x.dtype).reshape(B, T, N_BRANCH, D_MODEL)
    merged = (g[:, :, 0] * (o_a @ w_branch_a) + g[:, :, 1] * (o_b @ w_branch_b) + g[:, :, 2] * (o_c @ w_branch_c))
    x = x + merged @ w_out
    hn = _rmsnorm(x, ffn_norm_g)
    x = x + (jax.nn.silu(hn @ w_ffn_gate) * (hn @ w_ffn_up)) @ w_ffn_down
    return x, (lat_new, kr_new, k_new, v_new, ki_new, conv_new, h_new)


def _trunk(x, caches, weights, final_norm_g):
    B, T, _ = x.shape
    past_len = 0 if caches is None else caches[0].shape[2]
    pos = past_len + jnp.arange(T, dtype=jnp.int32)
    new = []
    for l in range(DEPTH):
        w_l = tuple(wt[l] for wt in weights)
        if caches is None:
            past = (None, None, None, None, None,
                    jnp.zeros((B, CONV_W - 1, LRU_WIDTH), x.dtype), jnp.zeros((B, LRU_WIDTH), jnp.float32))
        else:
            past = tuple(c[l] for c in caches)
        x, st = _layer(x, pos, past, w_l)
        new.append(st)
    y = _rmsnorm(x, final_norm_g)
    stacked = tuple(jnp.stack([s[i] for s in new]) for i in range(7))
    return y, stacked


def setup_inputs(seed: int = 0) -> dict:
    key = jax.random.key(seed)
    keys = jax.random.split(key, 48)
    counter = [0]

    def nxt():
        counter[0] += 1
        return keys[counter[0] - 1]

    def nrm(shape, scale):
        return jax.random.normal(nxt(), shape, jnp.float32) * scale

    def gain(shape):
        return 1.0 + 0.01 * jax.random.normal(nxt(), shape, jnp.float32)

    L = DEPTH
    base = jax.random.uniform(nxt(), (L, LRU_WIDTH), jnp.float32, 0.9, 0.999) ** (1.0 / LRU_C)
    lru_lambda = jnp.log(base) - jnp.log1p(-base)
    return {
        'x_prompt': nrm((BATCH, SEQ, D_MODEL), 1.0),
        'x_sample': nrm((DEC_BATCH, DEC_SEQ, D_MODEL), 1.0),
        'cache_mla_latent': nrm((L, DEC_BATCH, PAST_LEN, MLA_KV_LORA), 1.0),
        'cache_mla_krope': nrm((L, DEC_BATCH, PAST_LEN, MLA_ROPE), 1.0),
        'cache_dsa_k': nrm((L, DEC_BATCH, PAST_LEN, DSA_KV_HEADS, DSA_HD), 1.0),
        'cache_dsa_v': nrm((L, DEC_BATCH, PAST_LEN, DSA_KV_HEADS, DSA_HD), 1.0),
        'cache_dsa_kidx': nrm((L, DEC_BATCH, PAST_LEN, IDX_HD), 1.0),
        'state_lru_conv': nrm((L, DEC_BATCH, CONV_W - 1, LRU_WIDTH), 1.0),
        'state_lru_h': nrm((L, DEC_BATCH, LRU_WIDTH), 0.5),
        'attn_norm_g': gain((L, D_MODEL)),
        'w_in': nrm((L, D_MODEL, N_IN), D_MODEL ** -0.5),
        'b_gates': nrm((L, N_BRANCH * D_MODEL), 0.01),
        'lru_conv_w': nrm((L, CONV_W, LRU_WIDTH), CONV_W ** -0.5),
        'lru_conv_b': nrm((L, LRU_WIDTH), 0.01),
        'lru_wa': nrm((L, LRU_BLOCKS, LRU_BLOCK, LRU_BLOCK), LRU_BLOCK ** -0.5),
        'lru_ba': nrm((L, LRU_BLOCKS, LRU_BLOCK), 0.01),
        'lru_wx': nrm((L, LRU_BLOCKS, LRU_BLOCK, LRU_BLOCK), LRU_BLOCK ** -0.5),
        'lru_bx': nrm((L, LRU_BLOCKS, LRU_BLOCK), 0.01),
        'lru_lambda': lru_lambda,
        'mla_q_norm_g': gain((L, MLA_Q_LORA)),
        'mla_kv_norm_g': gain((L, MLA_KV_LORA)),
        'mla_w_q_up': nrm((L, MLA_Q_LORA, MLA_HEADS, MLA_NOPE + MLA_ROPE), MLA_Q_LORA ** -0.5),
        'mla_w_kv_up': nrm((L, MLA_KV_LORA, MLA_HEADS, MLA_NOPE + MLA_V), MLA_KV_LORA ** -0.5),
        'idx_k_norm_g': gain((L, IDX_HD)),
        'idx_k_norm_b': nrm((L, IDX_HD), 0.01),
        'w_branch_a': nrm((L, LRU_WIDTH, D_MODEL), LRU_WIDTH ** -0.5),
        'w_branch_b': nrm((L, MLA_HEADS * MLA_V, D_MODEL), (MLA_HEADS * MLA_V) ** -0.5),
        'w_branch_c': nrm((L, DSA_HEADS * DSA_HD, D_MODEL), (DSA_HEADS * DSA_HD) ** -0.5),
        'w_out': nrm((L, D_MODEL, D_MODEL), D_MODEL ** -0.5),
        'ffn_norm_g': gain((L, D_MODEL)),
        'w_ffn_gate': nrm((L, D_MODEL, D_FF), D_MODEL ** -0.5),
        'w_ffn_up': nrm((L, D_MODEL, D_FF), D_MODEL ** -0.5),
        'w_ffn_down': nrm((L, D_FF, D_MODEL), D_FF ** -0.5),
        'final_norm_g': gain((D_MODEL,)),
    }


def reference(x_prompt, x_sample, cache_mla_latent, cache_mla_krope, cache_dsa_k, cache_dsa_v, cache_dsa_kidx,
              state_lru_conv, state_lru_h, attn_norm_g, w_in, b_gates, lru_conv_w, lru_conv_b, lru_wa, lru_ba,
              lru_wx, lru_bx, lru_lambda, mla_q_norm_g, mla_kv_norm_g, mla_w_q_up, mla_w_kv_up, idx_k_norm_g,
              idx_k_norm_b, w_branch_a, w_branch_b, w_branch_c, w_out, ffn_norm_g, w_ffn_gate, w_ffn_up,
              w_ffn_down, final_norm_g):
    weights = (attn_norm_g, w_in, b_gates, lru_conv_w, lru_conv_b, lru_wa, lru_ba, lru_wx, lru_bx, lru_lambda,
               mla_q_norm_g, mla_kv_norm_g, mla_w_q_up, mla_w_kv_up, idx_k_norm_g, idx_k_norm_b,
               w_branch_a, w_branch_b, w_branch_c, w_out, ffn_norm_g, w_ffn_gate, w_ffn_up, w_ffn_down)
    y_prompt, st_p = _trunk(x_prompt, None, weights, final_norm_g)
    caches = (cache_mla_latent, cache_mla_krope, cache_dsa_k, cache_dsa_v, cache_dsa_kidx, state_lru_conv, state_lru_h)
    y_sample, st_s = _trunk(x_sample, caches, weights, final_norm_g)
    (lat_p, kr_p, k_p, v_p, ki_p, conv_p, h_p) = st_p
    (lat_s, kr_s, k_s, v_s, ki_s, conv_s, h_s) = st_s
    return (y_prompt, y_sample, lat_p, lat_s, kr_p, kr_s, k_p, k_s, v_p, v_s, ki_p, ki_s, conv_p, conv_s, h_p, h_s)
```

```python
import functools

import jax
import jax.numpy as jnp
from jax import lax
from jax.experimental import pallas as pl
from jax.experimental.pallas import tpu as pltpu

F32 = jnp.float32
BF16 = jnp.bfloat16
I32 = jnp.int32

D_MODEL = 1024
CHUNK_SHIFT = 6
ROPE_THETA = 10000.0
NORM_EPS = 1e-6
NEG_INF = -1e30
INT_MIN = -(2 ** 31)

LRU_WIDTH = 1024
LRU_BLOCKS = 8
LRU_BLOCK = 128
CONV_W = 4
LRU_C = 8.0

MLA_HEADS = 8
MLA_Q_LORA = 384
MLA_KV_LORA = 256
MLA_NOPE = 128
MLA_ROPE = 64
MLA_V = 128
MLA_QK_PAD = 256

DSA_HEADS = 8
DSA_KV_HEADS = 4
DSA_HD = 128
IDX_HEADS = 16
IDX_HD = 64
IDX_ROPE = 32
TOPK_MAX = 256
D_FF = 2816

LANES = 128
VMEM_LIMIT = 56 * 1024 * 1024

Z_A_GATE, Z_A_X, Z_C_Q, Z_C_IQ, Z_G = 0, 1024, 2048, 3072, 4096
Z_C_K, Z_C_V, Z_B_KV, Z_B_KR, Z_C_IK, Z_C_IW, Z_B_Q = 7168, 7680, 8192, 8448, 8576, 8704, 8832
Z_WIDTH = 9216


def _cparams(sem):
    return pltpu.CompilerParams(dimension_semantics=sem, vmem_limit_bytes=VMEM_LIMIT)


def _sigmoid(x):
    return 1.0 / (1.0 + jnp.exp(-x))


def _gelu_tanh(x):
    return 0.5 * x * (1.0 + jnp.tanh(0.7978845608028654 * (x + 0.044715 * (x * x * x))))


def _rms(x, g):
    return x * lax.rsqrt(jnp.mean(x * x, axis=-1, keepdims=True) + NORM_EPS) * g


def _row_tile(m, pref):
    t = min(m, pref)
    assert m % t == 0, (m, t)
    return t


def _norm_matmul_kernel(x_ref, g_ref, w_ref, o_ref, xn_ref):
    @pl.when(pl.program_id(1) == 0)
    def _():
        xn_ref[...] = _rms(x_ref[...], g_ref[...]).astype(BF16)

    o_ref[...] = jnp.dot(xn_ref[...], w_ref[...], preferred_element_type=F32)


def _norm_matmul(x, g, w):
    m, d = x.shape
    n = w.shape[1]
    tm, tn = _row_tile(m, 1024), 1024
    return pl.pallas_call(
        _norm_matmul_kernel,
        grid=(m // tm, n // tn),
        in_specs=[pl.BlockSpec((tm, d), lambda i, j: (i, 0)),
                  pl.BlockSpec((1, d), lambda i, j: (0, 0)),
                  pl.BlockSpec((d, tn), lambda i, j: (0, j))],
        out_specs=pl.BlockSpec((tm, tn), lambda i, j: (i, j)),
        out_shape=jax.ShapeDtypeStruct((m, n), F32),
        scratch_shapes=[pltpu.VMEM((tm, d), BF16)],
        compiler_params=_cparams(("parallel", "arbitrary")),
        name="norm_matmul",
    )(x, g.reshape(1, d), w)


def _lru_kernel(gate_ref, xin_ref, cbuf_ref, h0_ref, cw_ref, cb_ref, wa_ref, ba_ref, wx_ref, bx_ref, lam_ref,
                o_ref, clast_ref, hlast_ref, prev_ref, hc_ref, *, tt):
    @pl.when(pl.program_id(1) == 0)
    def _():
        prev_ref[...] = cbuf_ref[...]
        hc_ref[...] = h0_ref[...]

    row = lax.broadcasted_iota(I32, (tt, LRU_BLOCK), 0)
    row8 = lax.broadcasted_iota(I32, (8, LRU_BLOCK), 0)
    for n in range(LRU_BLOCKS):
        sl = slice(n * LRU_BLOCK, (n + 1) * LRU_BLOCK)
        x = xin_ref[:, sl]
        prev = prev_ref[:, sl]
        u = cb_ref[:, sl]
        for j in range(CONV_W):
            d = CONV_W - 1 - j
            if d == 0:
                xs = x
            else:
                rx = pltpu.roll(x, d, axis=0)
                head = jnp.where(row8 < d, pltpu.roll(prev, d, axis=0), rx[:8])
                xs = head if tt == 8 else jnp.concatenate([head, rx[8:]], axis=0)
            u = u + xs * cw_ref[j:j + 1, sl]
        ub = u.astype(BF16)
        r = _sigmoid(jnp.dot(ub, wa_ref[n], preferred_element_type=F32) + ba_ref[:, sl])
        ig = _sigmoid(jnp.dot(ub, wx_ref[n], preferred_element_type=F32) + bx_ref[:, sl])
        lam = lam_ref[:, sl]
        log_sig = jnp.minimum(lam, 0.0) - jnp.log1p(jnp.exp(-jnp.abs(lam)))
        a = jnp.exp(LRU_C * r * log_sig)
        b = jnp.sqrt(1.0 - a * a) * (ig * u)
        d = 1
        while d < tt:
            keep = row >= d
            b = jnp.where(keep, a * pltpu.roll(b, d, axis=0) + b, b)
            a = jnp.where(keep, a * pltpu.roll(a, d, axis=0), a)
            d *= 2
        h = a * hc_ref[:, sl] + b
        hc_ref[:, sl] = h[tt - 1:tt]
        o_ref[:, sl] = (h * _gelu_tanh(gate_ref[:, sl])).astype(o_ref.dtype)
        hlast_ref[:, sl] = h[tt - 8:]
        clast_ref[:, sl] = x[tt - 8:]
        prev_ref[:, sl] = x[tt - 8:]


def _lru(z3, conv_buf, h0, cw, cb, wa, ba, wx, bx, lam):
    b, t, _ = z3.shape
    w = LRU_WIDTH
    tt = _row_tile(t, 256)
    assert tt % 8 == 0 and tt & (tt - 1) == 0
    cbuf8 = jnp.concatenate([jnp.zeros((b, 8 - (CONV_W - 1), w), F32), conv_buf.astype(F32)], axis=1)
    cw8 = jnp.concatenate([cw, jnp.zeros((8 - CONV_W, w), F32)], axis=0)
    row = lambda v: v.reshape(1, w)
    full = lambda shape: pl.BlockSpec(shape, lambda bi, i: (0,) * len(shape))
    return pl.pallas_call(
        functools.partial(_lru_kernel, tt=tt),
        grid=(b, t // tt),
        in_specs=[pl.BlockSpec((None, tt, w), lambda bi, i: (bi, i, Z_A_GATE // w)),
                  pl.BlockSpec((None, tt, w), lambda bi, i: (bi, i, Z_A_X // w)),
                  pl.BlockSpec((None, 8, w), lambda bi, i: (bi, 0, 0)),
                  pl.BlockSpec((None, 1, w), lambda bi, i: (bi, 0, 0)),
                  full((8, w)), full((1, w)),
                  full((LRU_BLOCKS, LRU_BLOCK, LRU_BLOCK)), full((1, w)),
                  full((LRU_BLOCKS, LRU_BLOCK, LRU_BLOCK)), full((1, w)), full((1, w))],
        out_specs=[pl.BlockSpec((None, tt, w), lambda bi, i: (bi, i, 0)),
                   pl.BlockSpec((None, 8, w), lambda bi, i: (bi, 0, 0)),
                   pl.BlockSpec((None, 8, w), lambda bi, i: (bi, 0, 0))],
        out_shape=[jax.ShapeDtypeStruct((b, t, w), BF16),
                   jax.ShapeDtypeStruct((b, 8, w), F32),
                   jax.ShapeDtypeStruct((b, 8, w), F32)],
        scratch_shapes=[pltpu.VMEM((8, w), F32), pltpu.VMEM((1, w), F32)],
        compiler_params=_cparams(("parallel", "arbitrary")),
        name="lru",
    )(z3, z3, cbuf8, h0.astype(F32).reshape(b, 1, w), cw8, row(cb), wa.astype(BF16), row(ba),
      wx.astype(BF16), row(bx), row(lam))


def _rope_tables(pos, d, period, width=LANES):
    half = d // 2
    inv = ROPE_THETA ** (-jnp.arange(0, d, 2, dtype=F32) / d)
    ang = pos.astype(F32)[:, None] * inv[None, :]
    cos, sin = jnp.cos(ang), jnp.sin(ang)
    t = pos.shape[0]
    cos_p = jnp.concatenate([cos, cos, jnp.ones((t, period - d), F32)], axis=1)
    sin_p = jnp.concatenate([-sin, sin, jnp.zeros((t, period - d), F32)], axis=1)
    reps = width // period
    assert half * 2 == d and reps * period == width
    return jnp.tile(cos_p, (1, reps)), jnp.tile(sin_p, (1, reps))


def _rotate(x, cos_t, sin_t, half, period):
    if 2 * half == LANES:
        partner = pltpu.roll(x, half, axis=1)
    else:
        lane = lax.broadcasted_iota(I32, x.shape, 1)
        partner = jnp.where((lane & (period - 1)) < half,
                            pltpu.roll(x, LANES - half, axis=1), pltpu.roll(x, half, axis=1))
    return x * cos_t + partner * sin_t


def _mla_q_kernel(zq_ref, zkv_ref, zkr_ref, cos_ref, sin_ref, qg_ref, kvg_ref, wq_ref, q_ref, lat_ref, kr_ref):
    cos_t, sin_t = cos_ref[...], sin_ref[...]
    qn = _rms(zq_ref[...], qg_ref[...]).astype(BF16)
    q = jnp.dot(qn, wq_ref[...], preferred_element_type=F32)
    for h in range(MLA_HEADS):
        c0 = h * MLA_QK_PAD
        q_ref[:, c0:c0 + LANES] = q[:, c0:c0 + LANES].astype(BF16)
        q_ref[:, c0 + LANES:c0 + 2 * LANES] = _rotate(
            q[:, c0 + LANES:c0 + 2 * LANES], cos_t, sin_t, MLA_ROPE // 2, MLA_ROPE).astype(BF16)
    lat_ref[...] = _rms(zkv_ref[...], kvg_ref[...])
    kr_ref[...] = _rotate(zkr_ref[...], cos_t, sin_t, MLA_ROPE // 2, MLA_ROPE)


def _mla_q(z, cos_t, sin_t, qg, kvg, wq):
    m = z.shape[0]
    t = cos_t.shape[0]
    tm = _row_tile(t, 256)
    nt = t // tm
    zblk = lambda width, off: pl.BlockSpec((tm, width), lambda i: (i, off // width))
    tab = pl.BlockSpec((tm, LANES), lambda i: (i % nt, 0))
    full = lambda shape: pl.BlockSpec(shape, lambda i: (0,) * len(shape))
    return pl.pallas_call(
        _mla_q_kernel,
        grid=(m // tm,),
        in_specs=[zblk(MLA_Q_LORA, Z_B_Q), zblk(MLA_KV_LORA, Z_B_KV), zblk(LANES, Z_B_KR), tab, tab,
                  full((1, MLA_Q_LORA)), full((1, MLA_KV_LORA)), full((MLA_Q_LORA, MLA_HEADS * MLA_QK_PAD))],
        out_specs=[pl.BlockSpec((tm, MLA_HEADS * MLA_QK_PAD), lambda i: (i, 0)),
                   pl.BlockSpec((tm, MLA_KV_LORA), lambda i: (i, 0)),
                   pl.BlockSpec((tm, LANES), lambda i: (i, 0))],
        out_shape=[jax.ShapeDtypeStruct((m, MLA_HEADS * MLA_QK_PAD), BF16),
                   jax.ShapeDtypeStruct((m, MLA_KV_LORA), F32),
                   jax.ShapeDtypeStruct((m, LANES), F32)],
        compiler_params=_cparams(("parallel",)),
        name="mla_q",
    )(z, z, z, cos_t, sin_t, qg.reshape(1, -1), kvg.reshape(1, -1), wq)


def _mla_kv_kernel(lat_ref, kr_ref, wk_ref, wv_ref, k_ref, v_ref):
    latb = lat_ref[...].astype(BF16)
    k = jnp.dot(latb, wk_ref[...], preferred_element_type=F32)
    v_ref[...] = jnp.dot(latb, wv_ref[...], preferred_element_type=F32).astype(BF16)
    krb = kr_ref[...].astype(BF16)
    for h in range(MLA_HEADS):
        k_ref[:, h * MLA_QK_PAD:h * MLA_QK_PAD + LANES] = k[:, h * MLA_NOPE:(h + 1) * MLA_NOPE].astype(BF16)
        k_ref[:, h * MLA_QK_PAD + LANES:(h + 1) * MLA_QK_PAD] = krb


def _mla_kv(lat_all, kr_all, wk, wv):
    m = lat_all.shape[0]
    tm = 384 if m % 512 else 512
    assert m % tm == 0
    full = lambda shape: pl.BlockSpec(shape, lambda i: (0,) * len(shape))
    return pl.pallas_call(
        _mla_kv_kernel,
        grid=(m // tm,),
        in_specs=[pl.BlockSpec((tm, MLA_KV_LORA), lambda i: (i, 0)), pl.BlockSpec((tm, LANES), lambda i: (i, 0)),
                  full(wk.shape), full(wv.shape)],
        out_specs=[pl.BlockSpec((tm, MLA_HEADS * MLA_QK_PAD), lambda i: (i, 0)),
                   pl.BlockSpec((tm, MLA_HEADS * MLA_V), lambda i: (i, 0))],
        out_shape=[jax.ShapeDtypeStruct((m, MLA_HEADS * MLA_QK_PAD), BF16),
                   jax.ShapeDtypeStruct((m, MLA_HEADS * MLA_V), BF16)],
        compiler_params=_cparams(("parallel",)),
        name="mla_kv",
    )(lat_all, kr_all, wk, wv)


def _dsa_prep_kernel(zq_ref, zk_ref, ziq_ref, zik_ref, ziw_ref, cos_ref, sin_ref, cosp_ref, sinp_ref,
                     ikg_ref, ikb_ref, q_ref, k_ref, iq_ref, ki_ref, iw_ref):
    cos_t, sin_t = cos_ref[...], sin_ref[...]
    cos_p, sin_p = cosp_ref[...], sinp_ref[...]
    for h in range(DSA_HEADS):
        sl = slice(h * DSA_HD, (h + 1) * DSA_HD)
        q_ref[:, sl] = _rotate(zq_ref[:, sl], cos_t, sin_t, DSA_HD // 2, DSA_HD).astype(BF16)
    for h in range(DSA_KV_HEADS):
        sl = slice(h * DSA_HD, (h + 1) * DSA_HD)
        k_ref[:, sl] = _rotate(zk_ref[:, sl], cos_t, sin_t, DSA_HD // 2, DSA_HD)
    lane = lax.broadcasted_iota(I32, cos_t.shape, 1)
    low = lane < IDX_HD
    for c in range(IDX_HEADS // 2):
        y = _rotate(ziq_ref[:, c * LANES:(c + 1) * LANES], cos_p, sin_p, IDX_ROPE // 2, IDX_HD)
        iq_ref[2 * c] = jnp.where(low, y, 0.0).astype(BF16)
        iq_ref[2 * c + 1] = jnp.where(low, pltpu.roll(y, IDX_HD, axis=1), 0.0).astype(BF16)
    x = zik_ref[...]
    mean = jnp.sum(x, axis=-1, keepdims=True) * (1.0 / IDX_HD)
    xc = jnp.where(low, x - mean, 0.0)
    var = jnp.sum(xc * xc, axis=-1, keepdims=True) * (1.0 / IDX_HD)
    y = xc * lax.rsqrt(var + NORM_EPS) * ikg_ref[...] + ikb_ref[...]
    ki_ref[...] = _rotate(y, cos_p, sin_p, IDX_ROPE // 2, IDX_HD)
    iw_ref[...] = ziw_ref[...] * ((IDX_HEADS * IDX_HD) ** -0.5)


def _dsa_prep(z, tabs, ikg, ikb):
    m = z.shape[0]
    cos_t, sin_t, cos_p, sin_p = tabs
    t = cos_t.shape[0]
    tm = _row_tile(t, 256)
    nt = t // tm
    zblk = lambda width, off: pl.BlockSpec((tm, width), lambda i: (i, off // width))
    tab = pl.BlockSpec((tm, LANES), lambda i: (i % nt, 0))
    full = lambda shape: pl.BlockSpec(shape, lambda i: (0,) * len(shape))
    pad = lambda v: jnp.concatenate([v, jnp.zeros((LANES - IDX_HD,), F32)]).reshape(1, LANES)
    return pl.pallas_call(
        _dsa_prep_kernel,
        grid=(m // tm,),
        in_specs=[zblk(1024, Z_C_Q), zblk(512, Z_C_K), zblk(1024, Z_C_IQ), zblk(LANES, Z_C_IK), zblk(LANES, Z_C_IW),
                  tab, tab, tab, tab, full((1, LANES)), full((1, LANES))],
        out_specs=[pl.BlockSpec((tm, 1024), lambda i: (i, 0)),
                   pl.BlockSpec((tm, 512), lambda i: (i, 0)),
                   pl.BlockSpec((IDX_HEADS, tm, LANES), lambda i: (0, i, 0)),
                   pl.BlockSpec((tm, LANES), lambda i: (i, 0)),
                   pl.BlockSpec((tm, LANES), lambda i: (i, 0))],
        out_shape=[jax.ShapeDtypeStruct((m, 1024), BF16),
                   jax.ShapeDtypeStruct((m, 512), F32),
                   jax.ShapeDtypeStruct((IDX_HEADS, m, LANES), BF16),
                   jax.ShapeDtypeStruct((m, LANES), F32),
                   jax.ShapeDtypeStruct((m, LANES), F32)],
        compiler_params=_cparams(("parallel",)),
        name="dsa_prep",
    )(z, z, z, z, z, cos_t, sin_t, cos_p, sin_p, pad(ikg), pad(ikb))


def _visible_tiles(i, *, tq, tk, past, s_valid):
    last = past + (i + 1) * tq - 1
    vis_end = jnp.minimum(s_valid, ((last >> CHUNK_SHIFT) + 1) << CHUNK_SHIFT)
    return (vis_end + tk - 1) // tk


def _select_kernel(iq_ref, iw_ref, ki_ref, mask_ref, skey_ref, *, tq, tk, n_tiles, past, s_valid, topk):
    i = pl.program_id(1)
    nk = _visible_tiles(i, tq=tq, tk=tk, past=past, s_valid=s_valid)
    qchunk = (past + i * tq + lax.broadcasted_iota(I32, (tq, tk), 0)) >> CHUNK_SHIFT
    lane_k = lax.broadcasted_iota(I32, (tq, tk), 1)
    iw = iw_ref[...]
    wcols = [jnp.broadcast_to(iw[:, h:h + 1], (tq, LANES)) for h in range(IDX_HEADS)]

    def score_body(j, carry):
        kt = ki_ref[pl.ds(j * tk, tk), :]
        acc = [jnp.zeros((tq, LANES), F32) for _ in range(tk // LANES)]
        for h in range(IDX_HEADS):
            lg = lax.dot_general(iq_ref[h], kt, (((1,), (1,)), ((), ())), preferred_element_type=F32)
            for c in range(tk // LANES):
                acc[c] = acc[c] + wcols[h] * jnp.maximum(lg[:, c * LANES:(c + 1) * LANES], 0.0)
        score = jnp.concatenate(acc, axis=1)
        bits = lax.bitcast_convert_type(score, I32)
        key = bits ^ ((bits >> 31) & 0x7FFFFFFF)
        kpos = j * tk + lane_k
        vis = ((kpos >> CHUNK_SHIFT) <= qchunk) & (kpos < s_valid)
        skey_ref[:, pl.ds(j * tk, tk)] = jnp.where(vis, key, INT_MIN)
        return carry

    lax.fori_loop(0, nk, score_body, 0)

    def count_ge(cand):
        cand_b = jnp.broadcast_to(cand, (tq, LANES))

        def body(j, cnt):
            t = skey_ref[:, pl.ds(j * tk, tk)]
            for c in range(tk // LANES):
                cnt = cnt + jnp.where(t[:, c * LANES:(c + 1) * LANES] >= cand_b, 1, 0)
            return cnt

        cnt = lax.fori_loop(0, nk, body, jnp.zeros((tq, LANES), I32))
        return jnp.sum(cnt, axis=1, keepdims=True)

    prefix = jnp.where(count_ge(jnp.zeros((tq, 1), I32)) >= topk, 0, INT_MIN).astype(I32)

    def bit_body(it, prefix):
        cand = prefix + lax.shift_left(jnp.int32(1), 30 - it)
        return jnp.where(count_ge(cand) >= topk, cand, prefix)

    prefix = lax.fori_loop(0, 31, bit_body, prefix)
    thr = jnp.broadcast_to(jnp.maximum(prefix, INT_MIN + 1), (tq, LANES))

    def mask_body(j, carry):
        t = skey_ref[:, pl.ds(j * tk, tk)]
        cols = [jnp.where(t[:, c * LANES:(c + 1) * LANES] >= thr, 1, 0) for c in range(tk // LANES)]
        mask_ref[:, pl.ds(j * tk, tk)] = jnp.concatenate(cols, axis=1).astype(mask_ref.dtype)
        return carry

    lax.fori_loop(0, nk, mask_body, 0)

    def zero_body(j, carry):
        mask_ref[:, pl.ds(j * tk, tk)] = jnp.zeros((tq, tk), mask_ref.dtype)
        return carry

    lax.fori_loop(nk, n_tiles, zero_body, 0)


def _dsa_select(iq, iw, ki_all, *, b, t, past, s_valid, tk):
    s_pad = ki_all.shape[1]
    tq = _row_tile(t, 256)
    nq = t // tq
    topk = min(TOPK_MAX, s_valid // 4)
    kern = functools.partial(_select_kernel, tq=tq, tk=tk, n_tiles=s_pad // tk, past=past, s_valid=s_valid,
                             topk=topk)
    return pl.pallas_call(
        kern,
        grid=(b, nq),
        in_specs=[pl.BlockSpec((IDX_HEADS, tq, LANES), lambda bi, i: (0, bi * nq + i, 0)),
                  pl.BlockSpec((tq, LANES), lambda bi, i: (bi * nq + i, 0)),
                  pl.BlockSpec((None, s_pad, LANES), lambda bi, i: (bi, 0, 0))],
        out_specs=pl.BlockSpec((None, tq, s_pad), lambda bi, i: (bi, i, 0)),
        out_shape=jax.ShapeDtypeStruct((b, t, s_pad), MASK_DTYPE),
        scratch_shapes=[pltpu.VMEM((tq, s_pad), I32)],
        compiler_params=_cparams(("parallel", "parallel")),
        name="dsa_select",
    )(iq, iw, ki_all)


MASK_DTYPE = jnp.int8


def _attn_kernel(*refs, tq, tk, hpg, dqk, dv, past, s_valid, scale, has_mask):
    if has_mask:
        q_ref, k_ref, v_ref, mask_ref, o_ref, m_ref, l_ref, acc_ref = refs
    else:
        q_ref, k_ref, v_ref, o_ref, m_ref, l_ref, acc_ref = refs
    i = pl.program_id(2)
    nk = _visible_tiles(i, tq=tq, tk=tk, past=past, s_valid=s_valid)
    m_ref[...] = jnp.full(m_ref.shape, NEG_INF, F32)
    l_ref[...] = jnp.zeros(l_ref.shape, F32)
    acc_ref[...] = jnp.zeros(acc_ref.shape, F32)
    qchunk = (past + i * tq + lax.broadcasted_iota(I32, (tq, tk), 0)) >> CHUNK_SHIFT
    lane_k = lax.broadcasted_iota(I32, (tq, tk), 1)

    def body(j, carry):
        kt = k_ref[pl.ds(j * tk, tk), :]
        vt = v_ref[pl.ds(j * tk, tk), :]
        if has_mask:
            keep = mask_ref[:, pl.ds(j * tk, tk)].astype(I32) != 0
        else:
            kpos = j * tk + lane_k
            keep = ((kpos >> CHUNK_SHIFT) <= qchunk) & (kpos < s_valid)
        for h in range(hpg):
            q = q_ref[:, h * dqk:(h + 1) * dqk]
            s = lax.dot_general(q, kt, (((1,), (1,)), ((), ())), preferred_element_type=F32) * scale
            s = jnp.where(keep, s, NEG_INF)
            m_prev = m_ref[h]
            m_new = jnp.maximum(m_prev, jnp.max(s, axis=1, keepdims=True))
            alpha = jnp.exp(m_prev - m_new)
            p = jnp.exp(s - jnp.tile(m_new, (1, tk // LANES)))
            l_ref[h] = alpha * l_ref[h] + jnp.sum(p, axis=1, keepdims=True)
            acc_ref[h] = alpha * acc_ref[h] + jnp.dot(p.astype(BF16), vt, preferred_element_type=F32)
            m_ref[h] = m_new
        return carry

    lax.fori_loop(0, nk, body, 0)
    for h in range(hpg):
        o_ref[:, h * dv:(h + 1) * dv] = (acc_ref[h] / l_ref[h]).astype(o_ref.dtype)


def _attention(q, k, v, mask, *, groups, hpg, dqk, past, s_valid, tk, scale):
    b, t, _ = q.shape
    s_pad = k.shape[1]
    dv = LANES
    tq = _row_tile(t, 256)
    kern = functools.partial(_attn_kernel, tq=tq, tk=tk, hpg=hpg, dqk=dqk, dv=dv, past=past, s_valid=s_valid,
                             scale=scale, has_mask=mask is not None)
    in_specs = [pl.BlockSpec((None, tq, hpg * dqk), lambda bi, g, i: (bi, i, g)),
                pl.BlockSpec((None, s_pad, dqk), lambda bi, g, i: (bi, 0, g)),
                pl.BlockSpec((None, s_pad, dv), lambda bi, g, i: (bi, 0, g))]
    args = [q, k, v]
    if mask is not None:
        in_specs.append(pl.BlockSpec((None, tq, s_pad), lambda bi, g, i: (bi, i, 0)))
        args.append(mask)
    return pl.pallas_call(
        kern,
        grid=(b, groups, t // tq),
        in_specs=in_specs,
        out_specs=pl.BlockSpec((None, tq, hpg * dv), lambda bi, g, i: (bi, i, g)),
        out_shape=jax.ShapeDtypeStruct((b, t, groups * hpg * dv), BF16),
        scratch_shapes=[pltpu.VMEM((hpg, tq, LANES), F32), pltpu.VMEM((hpg, tq, LANES), F32),
                        pltpu.VMEM((hpg, tq, dv), F32)],
        compiler_params=_cparams(("parallel", "parallel", "arbitrary")),
        name="dsa_attention" if mask is not None else "mla_attention",
    )(*args)


def _merge_kernel(x_ref, oa_ref, ob_ref, oc_ref, g0_ref, g1_ref, g2_ref, bg_ref, wa_ref, wb_ref, wc_ref, wo_ref,
                  o_ref):
    merged = None
    for n, (o_r, g_r, w_r) in enumerate(((oa_ref, g0_ref, wa_ref), (ob_ref, g1_ref, wb_ref),
                                         (oc_ref, g2_ref, wc_ref))):
        gate = _sigmoid(g_r[...] + bg_ref[:, n * D_MODEL:(n + 1) * D_MODEL])
        term = gate * jnp.dot(o_r[...], w_r[...], preferred_element_type=F32)
        merged = term if merged is None else merged + term
    o_ref[...] = x_ref[...] + jnp.dot(merged.astype(BF16), wo_ref[...], preferred_element_type=F32)


def _merge(x, oa, ob, oc, z, bg, wa, wb, wc, wo):
    m, d = x.shape
    tm = _row_tile(m, 512)
    rows = pl.BlockSpec((tm, d), lambda i: (i, 0))
    gblk = lambda n: pl.BlockSpec((tm, d), lambda i: (i, Z_G // d + n))
    full = lambda shape: pl.BlockSpec(shape, lambda i: (0,) * len(shape))
    return pl.pallas_call(
        _merge_kernel,
        grid=(m // tm,),
        in_specs=[rows, rows, rows, rows, gblk(0), gblk(1), gblk(2), full((1, 3 * d)),
                  full((d, d)), full((d, d)), full((d, d)), full((d, d))],
        out_specs=rows,
        out_shape=jax.ShapeDtypeStruct((m, d), F32),
        compiler_params=_cparams(("parallel",)),
        name="merge",
    )(x, oa, ob, oc, z, z, z, bg.reshape(1, -1), wa, wb, wc, wo)


def _ffn_kernel(*refs, final):
    if final:
        x_ref, g_ref, wg_ref, wu_ref, wd_ref, gf_ref, o_ref, hn_ref, acc_ref = refs
    else:
        x_ref, g_ref, wg_ref, wu_ref, wd_ref, o_ref, hn_ref, acc_ref = refs
    j = pl.program_id(1)

    @pl.when(j == 0)
    def _():
        hn_ref[...] = _rms(x_ref[...], g_ref[...]).astype(BF16)
        acc_ref[...] = jnp.zeros(acc_ref.shape, F32)

    hn = hn_ref[...]
    gt = jnp.dot(hn, wg_ref[...], preferred_element_type=F32)
    up = jnp.dot(hn, wu_ref[...], preferred_element_type=F32)
    act = (gt * _sigmoid(gt) * up).astype(BF16)
    acc_ref[...] += jnp.dot(act, wd_ref[...], preferred_element_type=F32)

    @pl.when(j == pl.num_programs(1) - 1)
    def _():
        y = x_ref[...] + acc_ref[...]
        o_ref[...] = _rms(y, gf_ref[...]) if final else y


def _ffn(x, g, wg, wu, wd, gf=None):
    m, d = x.shape
    f = wg.shape[1]
    tm, tf = _row_tile(m, 512), f // 2
    final = gf is not None
    in_specs = [pl.BlockSpec((tm, d), lambda i, j: (i, 0)), pl.BlockSpec((1, d), lambda i, j: (0, 0)),
                pl.BlockSpec((d, tf), lambda i, j: (0, j)), pl.BlockSpec((d, tf), lambda i, j: (0, j)),
                pl.BlockSpec((tf, d), lambda i, j: (j, 0))]
    args = [x, g.reshape(1, d), wg, wu, wd]
    if final:
        in_specs.append(pl.BlockSpec((1, d), lambda i, j: (0, 0)))
        args.append(gf.reshape(1, d))
    return pl.pallas_call(
        functools.partial(_ffn_kernel, final=final),
        grid=(m // tm, f // tf),
        in_specs=in_specs,
        out_specs=pl.BlockSpec((tm, d), lambda i, j: (i, 0)),
        out_shape=jax.ShapeDtypeStruct((m, d), F32),
        scratch_shapes=[pltpu.VMEM((tm, d), BF16), pltpu.VMEM((tm, d), F32)],
        compiler_params=_cparams(("parallel", "arbitrary")),
        name="ffn",
    )(*args)


def _pad_cols(w, width):
    return jnp.concatenate([w, jnp.zeros(w.shape[:-1] + (width - w.shape[-1],), w.dtype)], axis=-1)


def _layout_w_in(w):
    a_gate, a_x, b_q, b_kv, b_kr = w[:, 0:1024], w[:, 1024:2048], w[:, 2048:2432], w[:, 2432:2688], w[:, 2688:2752]
    c_q, c_k, c_v, c_iq = w[:, 2752:3776], w[:, 3776:4288], w[:, 4288:4800], w[:, 4800:5824]
    c_ik, c_iw, g = w[:, 5824:5888], w[:, 5888:5904], w[:, 5904:8976]
    out = jnp.concatenate([a_gate, a_x, c_q, c_iq, g, c_k, c_v, b_kv, _pad_cols(b_kr, LANES),
                           _pad_cols(c_ik, LANES), _pad_cols(c_iw, LANES), b_q], axis=1)
    assert out.shape[1] == Z_WIDTH
    return out.astype(BF16)


def _layout_mla(w_q_up, w_kv_up):
    r = w_q_up.shape[0]
    zeros = jnp.zeros((r, MLA_HEADS, MLA_QK_PAD - MLA_NOPE - MLA_ROPE), w_q_up.dtype)
    wq = jnp.concatenate([w_q_up, zeros], axis=-1).reshape(r, MLA_HEADS * MLA_QK_PAD).astype(BF16)
    wk = w_kv_up[:, :, :MLA_NOPE].reshape(MLA_KV_LORA, MLA_HEADS * MLA_NOPE).astype(BF16)
    wv = w_kv_up[:, :, MLA_NOPE:].reshape(MLA_KV_LORA, MLA_HEADS * MLA_V).astype(BF16)
    return wq, wk, wv


def _key_tile(s_valid):
    return 512 if s_valid % 512 == 0 else 384


def _pad_keys(x, s_pad):
    b, s = x.shape[:2]
    if s == s_pad:
        return x
    return jnp.concatenate([x, jnp.zeros((b, s_pad - s) + x.shape[2:], x.dtype)], axis=1)


def _layer(x3, pos, past, w, final_g):
    (attn_norm_g, w_in, b_gates, lru_conv_w, lru_conv_b, lru_wa, lru_ba, lru_wx, lru_bx, lru_lambda,
     mla_q_norm_g, mla_kv_norm_g, mla_w_q_up, mla_w_kv_up, idx_k_norm_g, idx_k_norm_b,
     w_branch_a, w_branch_b, w_branch_c, w_out, ffn_norm_g, w_ffn_gate, w_ffn_up, w_ffn_down) = w
    past_lat, past_kr, past_k, past_v, past_ki, conv_buf, h0 = past
    b, t, d = x3.shape
    m = b * t
    past_len = 0 if past_lat is None else past_lat.shape[1]
    s_valid = past_len + t
    tk = _key_tile(s_valid)
    s_pad = -(-s_valid // tk) * tk
    x = x3.reshape(m, d)

    z = _norm_matmul(x, attn_norm_g, _layout_w_in(w_in))
    z3 = z.reshape(b, t, Z_WIDTH)

    o_a, conv8, h8 = _lru(z3, conv_buf, h0, lru_conv_w, lru_conv_b, lru_wa, lru_ba, lru_wx, lru_bx,
                          lru_lambda.reshape(-1))
    conv_new, h_new = conv8[:, 8 - (CONV_W - 1):], h8[:, 7]

    wq, wk, wv = _layout_mla(mla_w_q_up, mla_w_kv_up)
    cos64, sin64 = _rope_tables(pos, MLA_ROPE, MLA_ROPE)
    q_b, lat_new, kr_pad = _mla_q(z, cos64, sin64, mla_q_norm_g, mla_kv_norm_g, wq)
    lat_new = lat_new.reshape(b, t, MLA_KV_LORA)
    kr_pad = kr_pad.reshape(b, t, LANES)
    kr_new = kr_pad[:, :, :MLA_ROPE]
    if past_lat is None:
        lat_all, kr_all = lat_new, kr_pad
    else:
        lat_all = jnp.concatenate([past_lat.astype(F32), lat_new], axis=1)
        kr_all = jnp.concatenate([_pad_cols(past_kr.astype(F32), LANES), kr_pad], axis=1)
    lat_all, kr_all = _pad_keys(lat_all, s_pad), _pad_keys(kr_all, s_pad)
    k_b, v_b = _mla_kv(lat_all.reshape(b * s_pad, -1), kr_all.reshape(b * s_pad, -1), wk, wv)
    o_b = _attention(q_b.reshape(b, t, -1), k_b.reshape(b, s_pad, -1), v_b.reshape(b, s_pad, -1), None,
                     groups=MLA_HEADS, hpg=1, dqk=MLA_QK_PAD, past=past_len, s_valid=s_valid, tk=tk,
                     scale=(MLA_NOPE + MLA_ROPE) ** -0.5)

    tabs = _rope_tables(pos, DSA_HD, DSA_HD) + _rope_tables(pos, IDX_ROPE, IDX_HD)
    q_c, k_new, iq, ki_pad, iw = _dsa_prep(z, tabs, idx_k_norm_g, idx_k_norm_b)
    v_new = z3[:, :, Z_C_V:Z_C_V + DSA_KV_HEADS * DSA_HD]
    k_new = k_new.reshape(b, t, -1)
    ki_pad = ki_pad.reshape(b, t, LANES)
    ki_new = ki_pad[:, :, :IDX_HD]
    if past_k is None:
        k_all, v_all, ki_all = k_new, v_new, ki_pad
    else:
        k_all = jnp.concatenate([past_k.astype(F32).reshape(b, past_len, -1), k_new], axis=1)
        v_all = jnp.concatenate([past_v.astype(F32).reshape(b, past_len, -1), v_new], axis=1)
        ki_all = jnp.concatenate([_pad_cols(past_ki.astype(F32), LANES), ki_pad], axis=1)
    k_all, v_all, ki_all = (_pad_keys(a.astype(BF16), s_pad) for a in (k_all, v_all, ki_all))
    keep = _dsa_select(iq, iw, ki_all, b=b, t=t, past=past_len, s_valid=s_valid, tk=tk)
    o_c = _attention(q_c.reshape(b, t, -1), k_all, v_all, keep, groups=DSA_KV_HEADS,
                     hpg=DSA_HEADS // DSA_KV_HEADS, dqk=DSA_HD, past=past_len, s_valid=s_valid, tk=tk,
                     scale=DSA_HD ** -0.5)

    bf = lambda a: a.astype(BF16)
    x = _merge(x, o_a.reshape(m, -1), o_b.reshape(m, -1), o_c.reshape(m, -1), z, b_gates,
               bf(w_branch_a), bf(w_branch_b), bf(w_branch_c), bf(w_out))
    x = _ffn(x, ffn_norm_g, bf(w_ffn_gate), bf(w_ffn_up), bf(w_ffn_down), final_g)
    new = (lat_new, kr_new, k_new.reshape(b, t, DSA_KV_HEADS, DSA_HD), v_new.reshape(b, t, DSA_KV_HEADS, DSA_HD),
           ki_new, conv_new, h_new)
    return x.reshape(b, t, d), new


def _trunk(x, caches, weights, final_norm_g):
    b, t, _ = x.shape
    depth = weights[0].shape[0]
    past_len = 0 if caches is None else caches[0].shape[2]
    pos = past_len + jnp.arange(t, dtype=I32)
    new = []
    for l in range(depth):
        w_l = tuple(wt[l] for wt in weights)
        if caches is None:
            past = (None, None, None, None, None, jnp.zeros((b, CONV_W - 1, LRU_WIDTH), F32),
                    jnp.zeros((b, LRU_WIDTH), F32))
        else:
            past = tuple(c[l] for c in caches)
        x, st = _layer(x, pos, past, w_l, final_norm_g if l == depth - 1 else None)
        new.append(st)
    return x, tuple(jnp.stack([s[i] for s in new]) for i in range(7))


def kernel(x_prompt, x_sample, cache_mla_latent, cache_mla_krope, cache_dsa_k, cache_dsa_v, cache_dsa_kidx,
           state_lru_conv, state_lru_h, attn_norm_g, w_in, b_gates, lru_conv_w, lru_conv_b, lru_wa, lru_ba,
           lru_wx, lru_bx, lru_lambda, mla_q_norm_g, mla_kv_norm_g, mla_w_q_up, mla_w_kv_up, idx_k_norm_g,
           idx_k_norm_b, w_branch_a, w_branch_b, w_branch_c, w_out, ffn_norm_g, w_ffn_gate, w_ffn_up,
           w_ffn_down, final_norm_g):
    weights = (attn_norm_g, w_in, b_gates, lru_conv_w, lru_conv_b, lru_wa, lru_ba, lru_wx, lru_bx, lru_lambda,
               mla_q_norm_g, mla_kv_norm_g, mla_w_q_up, mla_w_kv_up, idx_k_norm_g, idx_k_norm_b,
               w_branch_a, w_branch_b, w_branch_c, w_out, ffn_norm_g, w_ffn_gate, w_ffn_up, w_ffn_down)
    y_p, st_p = _trunk(x_prompt, None, weights, final_norm_g)
    caches = (cache_mla_latent, cache_mla_krope, cache_dsa_k, cache_dsa_v, cache_dsa_kidx, state_lru_conv,
              state_lru_h)
    y_s, st_s = _trunk(x_sample, caches, weights, final_norm_g)
    (lat_p, kr_p, k_p, v_p, ki_p, conv_p, h_p) = st_p
    (lat_s, kr_s, k_s, v_s, ki_s, conv_s, h_s) = st_s
    return (y_p, y_s, lat_p, lat_s, kr_p, kr_s, k_p, k_s, v_p, v_s, ki_p, ki_s, conv_p, conv_s, h_p, h_s)
```

```python
import functools

import jax
import jax.numpy as jnp
from jax import lax
from jax.experimental import pallas as pl
from jax.experimental.pallas import tpu as pltpu

F32 = jnp.float32
BF16 = jnp.bfloat16
I32 = jnp.int32

D_MODEL = 1024
CHUNK_SHIFT = 6
ROPE_THETA = 10000.0
NORM_EPS = 1e-6
NEG_INF = -1e30
INT_MIN = -(2 ** 31)
LOG2_E = 1.4426950408889634
INDEX_BITS = 14
MASK_DTYPE = jnp.int8

LRU_WIDTH = 1024
LRU_BLOCKS = 8
LRU_BLOCK = 128
CONV_W = 4
LRU_C = 8.0

MLA_HEADS = 8
MLA_Q_LORA = 384
MLA_KV_LORA = 256
MLA_NOPE = 128
MLA_ROPE = 64
MLA_V = 128
MLA_QK_PAD = 256

DSA_HEADS = 8
DSA_KV_HEADS = 4
DSA_HD = 128
IDX_HEADS = 16
IDX_HD = 64
IDX_ROPE = 32
TOPK_MAX = 256
D_FF = 2816

LANES = 128
VMEM_LIMIT = 56 * 1024 * 1024

Z_A_GATE, Z_A_X, Z_C_Q, Z_C_IQ, Z_G = 0, 1024, 2048, 3072, 4096
Z_C_K, Z_C_V, Z_B_KV, Z_B_KR, Z_C_IK, Z_C_IW, Z_B_Q = 7168, 7680, 8192, 8448, 8576, 8704, 8832
Z_WIDTH = 9216


def _cparams(sem):
    return pltpu.CompilerParams(dimension_semantics=sem, vmem_limit_bytes=VMEM_LIMIT)


def _sigmoid(x):
    return 1.0 / (1.0 + jnp.exp(-x))


def _gelu_tanh(x):
    return 0.5 * x * (1.0 + jnp.tanh(0.7978845608028654 * (x + 0.044715 * (x * x * x))))


def _rms(x, g):
    return x * lax.rsqrt(jnp.mean(x * x, axis=-1, keepdims=True) + NORM_EPS) * g


def _row_tile(m, pref):
    t = min(m, pref)
    assert m % t == 0, (m, t)
    return t


def _norm_matmul_kernel(x_ref, g_ref, w_ref, o_ref, xn_ref):
    @pl.when(pl.program_id(1) == 0)
    def _():
        xn_ref[...] = _rms(x_ref[...], g_ref[...]).astype(BF16)

    o_ref[...] = jnp.dot(xn_ref[...], w_ref[...], preferred_element_type=F32)


def _norm_matmul(x, g, w):
    m, d = x.shape
    n = w.shape[1]
    tm, tn = _row_tile(m, 1024), 1024
    return pl.pallas_call(
        _norm_matmul_kernel,
        grid=(m // tm, n // tn),
        in_specs=[pl.BlockSpec((tm, d), lambda i, j: (i, 0)),
                  pl.BlockSpec((1, d), lambda i, j: (0, 0)),
                  pl.BlockSpec((d, tn), lambda i, j: (0, j))],
        out_specs=pl.BlockSpec((tm, tn), lambda i, j: (i, j)),
        out_shape=jax.ShapeDtypeStruct((m, n), F32),
        scratch_shapes=[pltpu.VMEM((tm, d), BF16)],
        compiler_params=_cparams(("parallel", "arbitrary")),
        name="norm_matmul",
    )(x, g.reshape(1, d), w)


def _lru_kernel(gate_ref, xin_ref, cbuf_ref, h0_ref, cw_ref, cb_ref, wa_ref, ba_ref, wx_ref, bx_ref, lam_ref,
                o_ref, clast_ref, hlast_ref, prev_ref, hc_ref, *, tt):
    @pl.when(pl.program_id(1) == 0)
    def _():
        prev_ref[...] = cbuf_ref[...]
        hc_ref[...] = h0_ref[...]

    row = lax.broadcasted_iota(I32, (tt, LRU_BLOCK), 0)
    row8 = lax.broadcasted_iota(I32, (8, LRU_BLOCK), 0)
    for n in range(LRU_BLOCKS):
        sl = slice(n * LRU_BLOCK, (n + 1) * LRU_BLOCK)
        x = xin_ref[:, sl]
        prev = prev_ref[:, sl]
        u = cb_ref[:, sl]
        for j in range(CONV_W):
            d = CONV_W - 1 - j
            if d == 0:
                xs = x
            else:
                rx = pltpu.roll(x, d, axis=0)
                head = jnp.where(row8 < d, pltpu.roll(prev, d, axis=0), rx[:8])
                xs = head if tt == 8 else jnp.concatenate([head, rx[8:]], axis=0)
            u = u + xs * cw_ref[j:j + 1, sl]
        ub = u.astype(BF16)
        r = _sigmoid(jnp.dot(ub, wa_ref[n], preferred_element_type=F32) + ba_ref[:, sl])
        ig = _sigmoid(jnp.dot(ub, wx_ref[n], preferred_element_type=F32) + bx_ref[:, sl])
        lam = lam_ref[:, sl]
        log_sig = jnp.minimum(lam, 0.0) - jnp.log1p(jnp.exp(-jnp.abs(lam)))
        a = jnp.exp(LRU_C * r * log_sig)
        b = jnp.sqrt(1.0 - a * a) * (ig * u)
        d = 1
        while d < tt:
            keep = row >= d
            b = jnp.where(keep, a * pltpu.roll(b, d, axis=0) + b, b)
            a = jnp.where(keep, a * pltpu.roll(a, d, axis=0), a)
            d *= 2
        h = a * hc_ref[:, sl] + b
        hc_ref[:, sl] = h[tt - 1:tt]
        o_ref[:, sl] = (h * _gelu_tanh(gate_ref[:, sl])).astype(o_ref.dtype)
        hlast_ref[:, sl] = h[tt - 8:]
        clast_ref[:, sl] = x[tt - 8:]
        prev_ref[:, sl] = x[tt - 8:]


def _lru(z3, conv_buf, h0, cw, cb, wa, ba, wx, bx, lam):
    b, t, _ = z3.shape
    w = LRU_WIDTH
    tt = _row_tile(t, 256)
    assert tt % 8 == 0 and tt & (tt - 1) == 0
    cbuf8 = jnp.concatenate([jnp.zeros((b, 8 - (CONV_W - 1), w), F32), conv_buf.astype(F32)], axis=1)
    cw8 = jnp.concatenate([cw, jnp.zeros((8 - CONV_W, w), F32)], axis=0)
    row = lambda v: v.reshape(1, w)
    full = lambda shape: pl.BlockSpec(shape, lambda bi, i: (0,) * len(shape))
    return pl.pallas_call(
        functools.partial(_lru_kernel, tt=tt),
        grid=(b, t // tt),
        in_specs=[pl.BlockSpec((None, tt, w), lambda bi, i: (bi, i, Z_A_GATE // w)),
                  pl.BlockSpec((None, tt, w), lambda bi, i: (bi, i, Z_A_X // w)),
                  pl.BlockSpec((None, 8, w), lambda bi, i: (bi, 0, 0)),
                  pl.BlockSpec((None, 1, w), lambda bi, i: (bi, 0, 0)),
                  full((8, w)), full((1, w)),
                  full((LRU_BLOCKS, LRU_BLOCK, LRU_BLOCK)), full((1, w)),
                  full((LRU_BLOCKS, LRU_BLOCK, LRU_BLOCK)), full((1, w)), full((1, w))],
        out_specs=[pl.BlockSpec((None, tt, w), lambda bi, i: (bi, i, 0)),
                   pl.BlockSpec((None, 8, w), lambda bi, i: (bi, 0, 0)),
                   pl.BlockSpec((None, 8, w), lambda bi, i: (bi, 0, 0))],
        out_shape=[jax.ShapeDtypeStruct((b, t, w), BF16),
                   jax.ShapeDtypeStruct((b, 8, w), F32),
                   jax.ShapeDtypeStruct((b, 8, w), F32)],
        scratch_shapes=[pltpu.VMEM((8, w), F32), pltpu.VMEM((1, w), F32)],
        compiler_params=_cparams(("parallel", "arbitrary")),
        name="lru",
    )(z3, z3, cbuf8, h0.astype(F32).reshape(b, 1, w), cw8, row(cb), wa.astype(BF16), row(ba),
      wx.astype(BF16), row(bx), row(lam))


def _rope_tables(pos, d, period, width=LANES):
    half = d // 2
    inv = ROPE_THETA ** (-jnp.arange(0, d, 2, dtype=F32) / d)
    ang = pos.astype(F32)[:, None] * inv[None, :]
    cos, sin = jnp.cos(ang), jnp.sin(ang)
    t = pos.shape[0]
    cos_p = jnp.concatenate([cos, cos, jnp.ones((t, period - d), F32)], axis=1)
    sin_p = jnp.concatenate([-sin, sin, jnp.zeros((t, period - d), F32)], axis=1)
    reps = width // period
    assert half * 2 == d and reps * period == width
    return jnp.tile(cos_p, (1, reps)), jnp.tile(sin_p, (1, reps))


def _rotate(x, cos_t, sin_t, half, period):
    if 2 * half == LANES:
        partner = pltpu.roll(x, half, axis=1)
    else:
        lane = lax.broadcasted_iota(I32, x.shape, 1)
        partner = jnp.where((lane & (period - 1)) < half,
                            pltpu.roll(x, LANES - half, axis=1), pltpu.roll(x, half, axis=1))
    return x * cos_t + partner * sin_t


def _mla_q_kernel(zq_ref, zkv_ref, zkr_ref, cos_ref, sin_ref, qg_ref, kvg_ref, wq_ref, q_ref, lat_ref, kr_ref):
    cos_t, sin_t = cos_ref[...], sin_ref[...]
    qn = _rms(zq_ref[...], qg_ref[...]).astype(BF16)
    q = jnp.dot(qn, wq_ref[...], preferred_element_type=F32)
    for h in range(MLA_HEADS):
        c0 = h * MLA_QK_PAD
        q_ref[:, c0:c0 + LANES] = q[:, c0:c0 + LANES].astype(BF16)
        q_ref[:, c0 + LANES:c0 + 2 * LANES] = _rotate(
            q[:, c0 + LANES:c0 + 2 * LANES], cos_t, sin_t, MLA_ROPE // 2, MLA_ROPE).astype(BF16)
    lat_ref[...] = _rms(zkv_ref[...], kvg_ref[...])
    kr_ref[...] = _rotate(zkr_ref[...], cos_t, sin_t, MLA_ROPE // 2, MLA_ROPE)


def _mla_q(z, cos_t, sin_t, qg, kvg, wq):
    m = z.shape[0]
    t = cos_t.shape[0]
    tm = _row_tile(t, 256)
    nt = t // tm
    zblk = lambda width, off: pl.BlockSpec((tm, width), lambda i: (i, off // width))
    tab = pl.BlockSpec((tm, LANES), lambda i: (i % nt, 0))
    full = lambda shape: pl.BlockSpec(shape, lambda i: (0,) * len(shape))
    return pl.pallas_call(
        _mla_q_kernel,
        grid=(m // tm,),
        in_specs=[zblk(MLA_Q_LORA, Z_B_Q), zblk(MLA_KV_LORA, Z_B_KV), zblk(LANES, Z_B_KR), tab, tab,
                  full((1, MLA_Q_LORA)), full((1, MLA_KV_LORA)), full((MLA_Q_LORA, MLA_HEADS * MLA_QK_PAD))],
        out_specs=[pl.BlockSpec((tm, MLA_HEADS * MLA_QK_PAD), lambda i: (i, 0)),
                   pl.BlockSpec((tm, MLA_KV_LORA), lambda i: (i, 0)),
                   pl.BlockSpec((tm, LANES), lambda i: (i, 0))],
        out_shape=[jax.ShapeDtypeStruct((m, MLA_HEADS * MLA_QK_PAD), BF16),
                   jax.ShapeDtypeStruct((m, MLA_KV_LORA), F32),
                   jax.ShapeDtypeStruct((m, LANES), F32)],
        compiler_params=_cparams(("parallel",)),
        name="mla_q",
    )(z, z, z, cos_t, sin_t, qg.reshape(1, -1), kvg.reshape(1, -1), wq)


def _mla_kv_kernel(lat_ref, kr_ref, wk_ref, wv_ref, k_ref, v_ref):
    latb = lat_ref[...].astype(BF16)
    k = jnp.dot(latb, wk_ref[...], preferred_element_type=F32)
    v_ref[...] = jnp.dot(latb, wv_ref[...], preferred_element_type=F32).astype(BF16)
    krb = kr_ref[...].astype(BF16)
    for h in range(MLA_HEADS):
        k_ref[:, h * MLA_QK_PAD:h * MLA_QK_PAD + LANES] = k[:, h * MLA_NOPE:(h + 1) * MLA_NOPE].astype(BF16)
        k_ref[:, h * MLA_QK_PAD + LANES:(h + 1) * MLA_QK_PAD] = krb


def _mla_kv(lat_all, kr_all, wk, wv):
    m = lat_all.shape[0]
    tm = 384 if m % 512 else 512
    assert m % tm == 0
    full = lambda shape: pl.BlockSpec(shape, lambda i: (0,) * len(shape))
    return pl.pallas_call(
        _mla_kv_kernel,
        grid=(m // tm,),
        in_specs=[pl.BlockSpec((tm, MLA_KV_LORA), lambda i: (i, 0)), pl.BlockSpec((tm, LANES), lambda i: (i, 0)),
                  full(wk.shape), full(wv.shape)],
        out_specs=[pl.BlockSpec((tm, MLA_HEADS * MLA_QK_PAD), lambda i: (i, 0)),
                   pl.BlockSpec((tm, MLA_HEADS * MLA_V), lambda i: (i, 0))],
        out_shape=[jax.ShapeDtypeStruct((m, MLA_HEADS * MLA_QK_PAD), BF16),
                   jax.ShapeDtypeStruct((m, MLA_HEADS * MLA_V), BF16)],
        compiler_params=_cparams(("parallel",)),
        name="mla_kv",
    )(lat_all, kr_all, wk, wv)


def _dsa_prep_kernel(zq_ref, zk_ref, ziq_ref, zik_ref, ziw_ref, cos_ref, sin_ref, cosp_ref, sinp_ref,
                     ikg_ref, ikb_ref, q_ref, k_ref, iq_ref, ki_ref, iw_ref):
    cos_t, sin_t = cos_ref[...], sin_ref[...]
    cos_p, sin_p = cosp_ref[...], sinp_ref[...]
    for h in range(DSA_HEADS):
        sl = slice(h * DSA_HD, (h + 1) * DSA_HD)
        q_ref[:, sl] = _rotate(zq_ref[:, sl], cos_t, sin_t, DSA_HD // 2, DSA_HD).astype(BF16)
    for h in range(DSA_KV_HEADS):
        sl = slice(h * DSA_HD, (h + 1) * DSA_HD)
        k_ref[:, sl] = _rotate(zk_ref[:, sl], cos_t, sin_t, DSA_HD // 2, DSA_HD)
    lane = lax.broadcasted_iota(I32, cos_t.shape, 1)
    low = lane < IDX_HD
    for c in range(IDX_HEADS // 2):
        y = _rotate(ziq_ref[:, c * LANES:(c + 1) * LANES], cos_p, sin_p, IDX_ROPE // 2, IDX_HD)
        iq_ref[2 * c] = jnp.where(low, y, 0.0).astype(BF16)
        iq_ref[2 * c + 1] = jnp.where(low, pltpu.roll(y, IDX_HD, axis=1), 0.0).astype(BF16)
    x = zik_ref[...]
    mean = jnp.sum(x, axis=-1, keepdims=True) * (1.0 / IDX_HD)
    xc = jnp.where(low, x - mean, 0.0)
    var = jnp.sum(xc * xc, axis=-1, keepdims=True) * (1.0 / IDX_HD)
    y = xc * lax.rsqrt(var + NORM_EPS) * ikg_ref[...] + ikb_ref[...]
    ki_ref[...] = _rotate(y, cos_p, sin_p, IDX_ROPE // 2, IDX_HD)
    iw_ref[...] = ziw_ref[...] * ((IDX_HEADS * IDX_HD) ** -0.5)


def _dsa_prep(z, tabs, ikg, ikb):
    m = z.shape[0]
    cos_t, sin_t, cos_p, sin_p = tabs
    t = cos_t.shape[0]
    tm = _row_tile(t, 256)
    nt = t // tm
    zblk = lambda width, off: pl.BlockSpec((tm, width), lambda i: (i, off // width))
    tab = pl.BlockSpec((tm, LANES), lambda i: (i % nt, 0))
    full = lambda shape: pl.BlockSpec(shape, lambda i: (0,) * len(shape))
    pad = lambda v: jnp.concatenate([v, jnp.zeros((LANES - IDX_HD,), F32)]).reshape(1, LANES)
    return pl.pallas_call(
        _dsa_prep_kernel,
        grid=(m // tm,),
        in_specs=[zblk(1024, Z_C_Q), zblk(512, Z_C_K), zblk(1024, Z_C_IQ), zblk(LANES, Z_C_IK), zblk(LANES, Z_C_IW),
                  tab, tab, tab, tab, full((1, LANES)), full((1, LANES))],
        out_specs=[pl.BlockSpec((tm, 1024), lambda i: (i, 0)),
                   pl.BlockSpec((tm, 512), lambda i: (i, 0)),
                   pl.BlockSpec((IDX_HEADS, tm, LANES), lambda i: (0, i, 0)),
                   pl.BlockSpec((tm, LANES), lambda i: (i, 0)),
                   pl.BlockSpec((tm, LANES), lambda i: (i, 0))],
        out_shape=[jax.ShapeDtypeStruct((m, 1024), BF16),
                   jax.ShapeDtypeStruct((m, 512), F32),
                   jax.ShapeDtypeStruct((IDX_HEADS, m, LANES), BF16),
                   jax.ShapeDtypeStruct((m, LANES), F32),
                   jax.ShapeDtypeStruct((m, LANES), F32)],
        compiler_params=_cparams(("parallel",)),
        name="dsa_prep",
    )(z, z, z, z, z, cos_t, sin_t, cos_p, sin_p, pad(ikg), pad(ikb))


def _visible_tiles(i, *, tq, tk, past, s_valid):
    last = past + (i + 1) * tq - 1
    vis_end = jnp.minimum(s_valid, ((last >> CHUNK_SHIFT) + 1) << CHUNK_SHIFT)
    return (vis_end + tk - 1) // tk


def _select_kernel(iq_ref, iw_ref, ki_ref, mask_ref, skey_ref, thr_ref, jcut_ref, *, tq, tk, n_tiles, past,
                   s_valid, topk):
    i = pl.program_id(1)
    nk = _visible_tiles(i, tq=tq, tk=tk, past=past, s_valid=s_valid)
    qchunk = (past + i * tq + lax.broadcasted_iota(I32, (tq, tk), 0)) >> CHUNK_SHIFT
    lane_k = lax.broadcasted_iota(I32, (tq, tk), 1)
    iw = iw_ref[...]
    wcols = [jnp.broadcast_to(iw[:, h:h + 1], (tq, LANES)) for h in range(IDX_HEADS)]

    def score_body(j, carry):
        kt = ki_ref[pl.ds(j * tk, tk), :]
        acc = [jnp.zeros((tq, LANES), F32) for _ in range(tk // LANES)]
        for h in range(IDX_HEADS):
            lg = lax.dot_general(iq_ref[h], kt, (((1,), (1,)), ((), ())), preferred_element_type=F32)
            for c in range(tk // LANES):
                acc[c] = acc[c] + wcols[h] * jnp.maximum(lg[:, c * LANES:(c + 1) * LANES], 0.0)
        score = jnp.concatenate(acc, axis=1)
        bits = lax.bitcast_convert_type(score, I32)
        key = bits ^ ((bits >> 31) & 0x7FFFFFFF)
        kpos = j * tk + lane_k
        vis = ((kpos >> CHUNK_SHIFT) <= qchunk) & (kpos < s_valid)
        skey_ref[:, pl.ds(j * tk, tk)] = jnp.where(vis, key, INT_MIN)
        return carry

    lax.fori_loop(0, nk, score_body, 0)

    rg = min(tq, 128)
    lane_g = lax.broadcasted_iota(I32, (rg, LANES), 1)

    def count(r0, pred):
        def body(j, cnt):
            t = skey_ref[r0:r0 + rg, pl.ds(j * tk, tk)]
            for c in range(tk // LANES):
                cnt = cnt + jnp.where(pred(t[:, c * LANES:(c + 1) * LANES], j * tk + c * LANES), 1, 0)
            return cnt

        cnt = lax.fori_loop(0, nk, body, jnp.zeros((rg, LANES), I32))
        return jnp.sum(cnt, axis=1, keepdims=True)

    def count_ge(r0, cand):
        cand_b = jnp.broadcast_to(cand, (rg, LANES))
        return count(r0, lambda t, k0: t >= cand_b)

    n_ties = jnp.zeros((1, 1), I32)
    for r0 in range(0, tq, rg):
        c0 = count_ge(r0, jnp.zeros((rg, 1), I32))
        nonneg = c0 >= topk
        prefix = jnp.where(nonneg, 0, INT_MIN).astype(I32)
        n_ge = jnp.where(nonneg, c0, nk * tk)

        def bit_body(it, carry, r0=r0):
            prefix, n_ge = carry
            cand = prefix + lax.shift_left(jnp.int32(1), 30 - it)
            c = count_ge(r0, cand)
            ok = c >= topk
            return jnp.where(ok, cand, prefix), jnp.where(ok, c, n_ge)

        prefix, n_ge = lax.fori_loop(0, 31, bit_body, (prefix, n_ge))
        thr_ref[r0:r0 + rg] = jnp.broadcast_to(jnp.maximum(prefix, INT_MIN + 1), (rg, LANES))
        tied = (n_ge > topk) & (prefix > INT_MIN)
        n_ties = n_ties + jnp.sum(jnp.where(tied, 1, 0), axis=0, keepdims=True)
    jcut_ref[...] = jnp.full(jcut_ref.shape, n_tiles * tk, I32)

    @pl.when(n_ties[0, 0] > 0)
    def _():
        for r0 in range(0, tq, rg):
            thr = thr_ref[r0:r0 + rg]
            need = topk - count(r0, lambda t, k0: t > thr)

            def idx_body(it, jcut, r0=r0, thr=thr, need=need):
                cand = jcut + lax.shift_left(jnp.int32(1), INDEX_BITS - 1 - it)
                cand_b = jnp.broadcast_to(cand, (rg, LANES))
                c = count(r0, lambda t, k0: (t == thr) & (k0 + lane_g < cand_b))
                return jnp.where(c < need, cand, jcut)

            jcut = lax.fori_loop(0, INDEX_BITS, idx_body, jnp.zeros((rg, 1), I32))
            jcut_ref[r0:r0 + rg] = jnp.broadcast_to(jcut, (rg, LANES))

    thr_all, jcut_all = thr_ref[...], jcut_ref[...]
    lane_q = lax.broadcasted_iota(I32, (tq, LANES), 1)

    def mask_body(j, carry):
        t = skey_ref[:, pl.ds(j * tk, tk)]
        cols = []
        for c in range(tk // LANES):
            tc = t[:, c * LANES:(c + 1) * LANES]
            kidx = j * tk + c * LANES + lane_q
            cols.append(jnp.where((tc > thr_all) | ((tc == thr_all) & (kidx <= jcut_all)), 1, 0))
        mask_ref[:, pl.ds(j * tk, tk)] = jnp.concatenate(cols, axis=1).astype(mask_ref.dtype)
        return carry

    lax.fori_loop(0, nk, mask_body, 0)

    def zero_body(j, carry):
        mask_ref[:, pl.ds(j * tk, tk)] = jnp.zeros((tq, tk), mask_ref.dtype)
        return carry

    lax.fori_loop(nk, n_tiles, zero_body, 0)


def _select_cols_kernel(iq_ref, iwt_ref, ki_ref, mask_ref, skey_ref, jcut_ref, *, tq, tk, n_tiles, past, s_valid,
                        topk):
    i = pl.program_id(1)
    nk = _visible_tiles(i, tq=tq, tk=tk, past=past, s_valid=s_valid)
    sub = 32
    qchunk = (past + i * tq + lax.broadcasted_iota(I32, (tk, tq), 1)) >> CHUNK_SHIFT
    row_k = lax.broadcasted_iota(I32, (tk, tq), 0)
    row_s = lax.broadcasted_iota(I32, (sub, tq), 0)

    def score_body(j, carry):
        kt = ki_ref[pl.ds(j * tk, tk), :]
        acc = jnp.zeros((tk, tq), F32)
        for h in range(IDX_HEADS):
            lg = lax.dot_general(kt, iq_ref[h], (((1,), (1,)), ((), ())), preferred_element_type=F32)
            acc = acc + iwt_ref[h:h + 1, :] * jnp.maximum(lg, 0.0)
        bits = lax.bitcast_convert_type(acc, I32)
        key = bits ^ ((bits >> 31) & 0x7FFFFFFF)
        kpos = j * tk + row_k
        vis = ((kpos >> CHUNK_SHIFT) <= qchunk) & (kpos < s_valid)
        skey_ref[pl.ds(j * tk, tk), :] = jnp.where(vis, key, INT_MIN)
        return carry

    lax.fori_loop(0, nk, score_body, 0)

    def count(pred):
        def body(j, cnt):
            for r in range(tk // sub):
                k0 = j * tk + r * sub
                cnt = cnt + jnp.where(pred(skey_ref[pl.ds(k0, sub), :], k0), 1, 0)
            return cnt

        cnt = lax.fori_loop(0, nk, body, jnp.zeros((sub, tq), I32))
        return jnp.sum(cnt, axis=0, keepdims=True)

    def count_ge(cand):
        cand_b = jnp.broadcast_to(cand, (sub, tq))
        return count(lambda t, k0: t >= cand_b)

    c0 = count_ge(jnp.zeros((1, tq), I32))
    nonneg = c0 >= topk
    prefix = jnp.where(nonneg, 0, INT_MIN).astype(I32)
    n_ge = jnp.where(nonneg, c0, nk * tk)

    def bit_body(it, carry):
        prefix, n_ge = carry
        cand = prefix + lax.shift_left(jnp.int32(1), 30 - it)
        c = count_ge(cand)
        ok = c >= topk
        return jnp.where(ok, cand, prefix), jnp.where(ok, c, n_ge)

    prefix, n_ge = lax.fori_loop(0, 31, bit_body, (prefix, n_ge))
    thr = jnp.maximum(prefix, INT_MIN + 1)
    tied = (n_ge > topk) & (prefix > INT_MIN)
    n_ties = jnp.sum(jnp.where(tied, 1, 0), axis=1, keepdims=True)
    jcut_ref[...] = jnp.full(jcut_ref.shape, n_tiles * tk, I32)

    @pl.when(n_ties[0, 0] > 0)
    def _():
        thr_b = jnp.broadcast_to(thr, (sub, tq))
        need = topk - count(lambda t, k0: t > thr_b)

        def idx_body(it, jcut):
            cand_b = jnp.broadcast_to(jcut + lax.shift_left(jnp.int32(1), INDEX_BITS - 1 - it), (sub, tq))
            c = count(lambda t, k0: (t == thr_b) & (k0 + row_s < cand_b))
            return jnp.where(c < need, cand_b[:1], jcut)

        jcut = lax.fori_loop(0, INDEX_BITS, idx_body, jnp.zeros((1, tq), I32))
        jcut_ref[...] = jnp.broadcast_to(jcut, jcut_ref.shape)

    thr_t = jnp.broadcast_to(thr, (tk, tq))
    jcut_t = jnp.broadcast_to(jcut_ref[:1], (tk, tq))

    def mask_body(j, carry):
        t = skey_ref[pl.ds(j * tk, tk), :]
        keep = (t > thr_t) | ((t == thr_t) & (j * tk + row_k <= jcut_t))
        mask_ref[:, pl.ds(j * tk, tk)] = jnp.where(keep, 1, 0).T.astype(mask_ref.dtype)
        return carry

    lax.fori_loop(0, nk, mask_body, 0)

    def zero_body(j, carry):
        mask_ref[:, pl.ds(j * tk, tk)] = jnp.zeros((tq, tk), mask_ref.dtype)
        return carry

    lax.fori_loop(nk, n_tiles, zero_body, 0)


def _dsa_select(iq, iw, ki_all, *, b, t, past, s_valid, tk):
    s_pad = ki_all.shape[1]
    tq = _row_tile(t, 256)
    nq = t // tq
    topk = min(TOPK_MAX, s_valid // 4)
    assert s_pad < 2 ** INDEX_BITS
    if tq % LANES == 0:
        kern = functools.partial(_select_cols_kernel, tq=tq, tk=tk, n_tiles=s_pad // tk, past=past,
                                 s_valid=s_valid, topk=topk)
        iwt = iw[:, :IDX_HEADS].reshape(b, t, IDX_HEADS).transpose(0, 2, 1)
        return pl.pallas_call(
            kern,
            grid=(b, nq),
            in_specs=[pl.BlockSpec((IDX_HEADS, tq, LANES), lambda bi, i: (0, bi * nq + i, 0)),
                      pl.BlockSpec((None, IDX_HEADS, tq), lambda bi, i: (bi, 0, i)),
                      pl.BlockSpec((None, s_pad, LANES), lambda bi, i: (bi, 0, 0))],
            out_specs=pl.BlockSpec((None, tq, s_pad), lambda bi, i: (bi, i, 0)),
            out_shape=jax.ShapeDtypeStruct((b, t, s_pad), MASK_DTYPE),
            scratch_shapes=[pltpu.VMEM((s_pad, tq), I32), pltpu.VMEM((8, tq), I32)],
            compiler_params=_cparams(("parallel", "parallel")),
            name="dsa_select",
        )(iq, iwt, ki_all)
    kern = functools.partial(_select_kernel, tq=tq, tk=tk, n_tiles=s_pad // tk, past=past, s_valid=s_valid,
                             topk=topk)
    return pl.pallas_call(
        kern,
        grid=(b, nq),
        in_specs=[pl.BlockSpec((IDX_HEADS, tq, LANES), lambda bi, i: (0, bi * nq + i, 0)),
                  pl.BlockSpec((tq, LANES), lambda bi, i: (bi * nq + i, 0)),
                  pl.BlockSpec((None, s_pad, LANES), lambda bi, i: (bi, 0, 0))],
        out_specs=pl.BlockSpec((None, tq, s_pad), lambda bi, i: (bi, i, 0)),
        out_shape=jax.ShapeDtypeStruct((b, t, s_pad), MASK_DTYPE),
        scratch_shapes=[pltpu.VMEM((tq, s_pad), I32), pltpu.VMEM((tq, LANES), I32), pltpu.VMEM((tq, LANES), I32)],
        compiler_params=_cparams(("parallel", "parallel")),
        name="dsa_select",
    )(iq, iw, ki_all)


def _attn_kernel(*refs, tq, tk, hpg, dqk, dv, past, s_valid, scale, has_mask, kv_shared):
    if has_mask:
        q_ref, k_ref, v_ref, mask_ref, o_ref, m_ref, acc_ref, sa_ref, sb_ref = refs
    else:
        q_ref, k_ref, v_ref, o_ref, m_ref, acc_ref, sa_ref, sb_ref, vis_ref = refs
    i = pl.program_id(2)
    nk = _visible_tiles(i, tq=tq, tk=tk, past=past, s_valid=s_valid)
    m_ref[...] = jnp.full(m_ref.shape, NEG_INF, F32)
    acc_ref[...] = jnp.zeros(acc_ref.shape, F32)
    lane_k = lax.broadcasted_iota(I32, (tq, tk), 1)
    if not has_mask:
        qchunk = (past + i * tq + lax.broadcasted_iota(I32, (tq, tk), 0)) >> CHUNK_SHIFT
        vis_ref[...] = qchunk - (lane_k >> CHUNK_SHIFT)
    c = scale * LOG2_E
    ones_col = jnp.where(lax.broadcasted_iota(I32, (tk, LANES), 1) == 0, 1.0, 0.0).astype(BF16)

    def scores(j, s_ref):
        rows = pl.ds(j * tk, tk)
        for h in range(hpg):
            hk = 0 if kv_shared else h
            kt = k_ref[rows, hk * dqk:(hk + 1) * dqk]
            q = q_ref[:, h * dqk:(h + 1) * dqk]
            s = lax.dot_general(q, kt, (((1,), (1,)), ((), ())), preferred_element_type=F32)
            s_ref[h] = s * c

    def softmax_pv(j, s_ref):
        rows = pl.ds(j * tk, tk)
        if has_mask:
            keep = mask_ref[:, rows].astype(I32) != 0
        else:
            keep = vis_ref[...] >= ((j * tk) >> CHUNK_SHIFT)
            if s_valid % tk:
                keep = keep & (j * tk + lane_k < s_valid)
        for h in range(hpg):
            hk = 0 if kv_shared else h
            v1 = jnp.concatenate([v_ref[rows, hk * dv:(hk + 1) * dv], ones_col], axis=1)
            s = jnp.where(keep, s_ref[h], NEG_INF)
            m_prev = m_ref[h]
            m_new = jnp.maximum(m_prev, jnp.max(s, axis=1, keepdims=True))
            alpha = jnp.exp2(m_prev - m_new)
            p = jnp.exp2(s - jnp.tile(m_new, (1, tk // LANES)))
            pv = jnp.dot(p.astype(BF16), v1, preferred_element_type=F32)
            acc_ref[h] = jnp.tile(alpha, (1, 2)) * acc_ref[h] + pv
            m_ref[h] = m_new

    def pair_body(jj, carry):
        j0 = 2 * jj
        scores(j0 + 1, sb_ref)
        softmax_pv(j0, sa_ref)
        scores(j0 + 2, sa_ref)
        softmax_pv(j0 + 1, sb_ref)
        return carry

    scores(0, sa_ref)
    n_pairs = (nk - 1) // 2
    lax.fori_loop(0, n_pairs, pair_body, 0)
    j0 = 2 * n_pairs

    @pl.when(nk - j0 == 2)
    def _():
        scores(j0 + 1, sb_ref)
        softmax_pv(j0, sa_ref)
        softmax_pv(j0 + 1, sb_ref)

    @pl.when(nk - j0 == 1)
    def _():
        softmax_pv(j0, sa_ref)

    for h in range(hpg):
        acc = acc_ref[h]
        o_ref[:, h * dv:(h + 1) * dv] = (acc[:, :dv] / acc[:, dv:dv + 1]).astype(o_ref.dtype)


def _attention(q, k, v, mask, *, groups, hpg, dqk, kv_shared, past, s_valid, tk, scale):
    b, t, _ = q.shape
    s_pad = k.shape[1]
    dv = LANES
    kvh = 1 if kv_shared else hpg
    tq = _row_tile(t, 256)
    kern = functools.partial(_attn_kernel, tq=tq, tk=tk, hpg=hpg, dqk=dqk, dv=dv, past=past, s_valid=s_valid,
                             scale=scale, has_mask=mask is not None, kv_shared=kv_shared)
    in_specs = [pl.BlockSpec((None, tq, hpg * dqk), lambda bi, g, i: (bi, i, g)),
                pl.BlockSpec((None, s_pad, kvh * dqk), lambda bi, g, i: (bi, 0, g)),
                pl.BlockSpec((None, s_pad, kvh * dv), lambda bi, g, i: (bi, 0, g))]
    args = [q, k, v]
    if mask is not None:
        in_specs.append(pl.BlockSpec((None, tq, s_pad), lambda bi, g, i: (bi, i, 0)))
        args.append(mask)
    return pl.pallas_call(
        kern,
        grid=(b, groups, t // tq),
        in_specs=in_specs,
        out_specs=pl.BlockSpec((None, tq, hpg * dv), lambda bi, g, i: (bi, i, g)),
        out_shape=jax.ShapeDtypeStruct((b, t, groups * hpg * dv), BF16),
        scratch_shapes=[pltpu.VMEM((hpg, tq, LANES), F32), pltpu.VMEM((hpg, tq, 2 * dv), F32),
                        pltpu.VMEM((hpg, tq, tk), F32), pltpu.VMEM((hpg, tq, tk), F32)]
        + ([] if mask is not None else [pltpu.VMEM((tq, tk), I32)]),
        compiler_params=_cparams(("parallel", "parallel", "arbitrary")),
        name="dsa_attention" if mask is not None else "mla_attention",
    )(*args)


def _merge_kernel(x_ref, oa_ref, ob_ref, oc_ref, g0_ref, g1_ref, g2_ref, bg_ref, wa_ref, wb_ref, wc_ref, wo_ref,
                  o_ref):
    merged = None
    for n, (o_r, g_r, w_r) in enumerate(((oa_ref, g0_ref, wa_ref), (ob_ref, g1_ref, wb_ref),
                                         (oc_ref, g2_ref, wc_ref))):
        gate = _sigmoid(g_r[...] + bg_ref[:, n * D_MODEL:(n + 1) * D_MODEL])
        term = gate * jnp.dot(o_r[...], w_r[...], preferred_element_type=F32)
        merged = term if merged is None else merged + term
    o_ref[...] = x_ref[...] + jnp.dot(merged.astype(BF16), wo_ref[...], preferred_element_type=F32)


def _merge(x, oa, ob, oc, z, bg, wa, wb, wc, wo):
    m, d = x.shape
    tm = _row_tile(m, 512)
    rows = pl.BlockSpec((tm, d), lambda i: (i, 0))
    gblk = lambda n: pl.BlockSpec((tm, d), lambda i: (i, Z_G // d + n))
    full = lambda shape: pl.BlockSpec(shape, lambda i: (0,) * len(shape))
    return pl.pallas_call(
        _merge_kernel,
        grid=(m // tm,),
        in_specs=[rows, rows, rows, rows, gblk(0), gblk(1), gblk(2), full((1, 3 * d)),
                  full((d, d)), full((d, d)), full((d, d)), full((d, d))],
        out_specs=rows,
        out_shape=jax.ShapeDtypeStruct((m, d), F32),
        compiler_params=_cparams(("parallel",)),
        name="merge",
    )(x, oa, ob, oc, z, z, z, bg.reshape(1, -1), wa, wb, wc, wo)


def _ffn_kernel(*refs, final):
    if final:
        x_ref, g_ref, wg_ref, wu_ref, wd_ref, gf_ref, o_ref, hn_ref, acc_ref = refs
    else:
        x_ref, g_ref, wg_ref, wu_ref, wd_ref, o_ref, hn_ref, acc_ref = refs
    j = pl.program_id(1)

    @pl.when(j == 0)
    def _():
        hn_ref[...] = _rms(x_ref[...], g_ref[...]).astype(BF16)
        acc_ref[...] = jnp.zeros(acc_ref.shape, F32)

    hn = hn_ref[...]
    gt = jnp.dot(hn, wg_ref[...], preferred_element_type=F32)
    up = jnp.dot(hn, wu_ref[...], preferred_element_type=F32)
    act = (gt * _sigmoid(gt) * up).astype(BF16)
    acc_ref[...] += jnp.dot(act, wd_ref[...], preferred_element_type=F32)

    @pl.when(j == pl.num_programs(1) - 1)
    def _():
        y = x_ref[...] + acc_ref[...]
        o_ref[...] = _rms(y, gf_ref[...]) if final else y


def _ffn(x, g, wg, wu, wd, gf=None):
    m, d = x.shape
    f = wg.shape[1]
    tm, tf = _row_tile(m, 512), f // 2
    final = gf is not None
    in_specs = [pl.BlockSpec((tm, d), lambda i, j: (i, 0)), pl.BlockSpec((1, d), lambda i, j: (0, 0)),
                pl.BlockSpec((d, tf), lambda i, j: (0, j)), pl.BlockSpec((d, tf), lambda i, j: (0, j)),
                pl.BlockSpec((tf, d), lambda i, j: (j, 0))]
    args = [x, g.reshape(1, d), wg, wu, wd]
    if final:
        in_specs.append(pl.BlockSpec((1, d), lambda i, j: (0, 0)))
        args.append(gf.reshape(1, d))
    return pl.pallas_call(
        functools.partial(_ffn_kernel, final=final),
        grid=(m // tm, f // tf),
        in_specs=in_specs,
        out_specs=pl.BlockSpec((tm, d), lambda i, j: (i, 0)),
        out_shape=jax.ShapeDtypeStruct((m, d), F32),
        scratch_shapes=[pltpu.VMEM((tm, d), BF16), pltpu.VMEM((tm, d), F32)],
        compiler_params=_cparams(("parallel", "arbitrary")),
        name="ffn",
    )(*args)


def _pad_cols(w, width):
    return jnp.concatenate([w, jnp.zeros(w.shape[:-1] + (width - w.shape[-1],), w.dtype)], axis=-1)


def _layout_w_in(w):
    a_gate, a_x, b_q, b_kv, b_kr = w[:, 0:1024], w[:, 1024:2048], w[:, 2048:2432], w[:, 2432:2688], w[:, 2688:2752]
    c_q, c_k, c_v, c_iq = w[:, 2752:3776], w[:, 3776:4288], w[:, 4288:4800], w[:, 4800:5824]
    c_ik, c_iw, g = w[:, 5824:5888], w[:, 5888:5904], w[:, 5904:8976]
    out = jnp.concatenate([a_gate, a_x, c_q, c_iq, g, c_k, c_v, b_kv, _pad_cols(b_kr, LANES),
                           _pad_cols(c_ik, LANES), _pad_cols(c_iw, LANES), b_q], axis=1)
    assert out.shape[1] == Z_WIDTH
    return out.astype(BF16)


def _layout_mla(w_q_up, w_kv_up):
    r = w_q_up.shape[0]
    zeros = jnp.zeros((r, MLA_HEADS, MLA_QK_PAD - MLA_NOPE - MLA_ROPE), w_q_up.dtype)
    wq = jnp.concatenate([w_q_up, zeros], axis=-1).reshape(r, MLA_HEADS * MLA_QK_PAD).astype(BF16)
    wk = w_kv_up[:, :, :MLA_NOPE].reshape(MLA_KV_LORA, MLA_HEADS * MLA_NOPE).astype(BF16)
    wv = w_kv_up[:, :, MLA_NOPE:].reshape(MLA_KV_LORA, MLA_HEADS * MLA_V).astype(BF16)
    return wq, wk, wv


def _key_tile(s_valid):
    return 512 if s_valid % 512 == 0 else 384


def _pad_keys(x, s_pad):
    b, s = x.shape[:2]
    if s == s_pad:
        return x
    return jnp.concatenate([x, jnp.zeros((b, s_pad - s) + x.shape[2:], x.dtype)], axis=1)


def _layer(x3, pos, past, w, final_g):
    (attn_norm_g, w_in, b_gates, lru_conv_w, lru_conv_b, lru_wa, lru_ba, lru_wx, lru_bx, lru_lambda,
     mla_q_norm_g, mla_kv_norm_g, mla_w_q_up, mla_w_kv_up, idx_k_norm_g, idx_k_norm_b,
     w_branch_a, w_branch_b, w_branch_c, w_out, ffn_norm_g, w_ffn_gate, w_ffn_up, w_ffn_down) = w
    past_lat, past_kr, past_k, past_v, past_ki, conv_buf, h0 = past
    b, t, d = x3.shape
    m = b * t
    past_len = 0 if past_lat is None else past_lat.shape[1]
    s_valid = past_len + t
    tk = _key_tile(s_valid)
    s_pad = -(-s_valid // tk) * tk
    x = x3.reshape(m, d)

    z = _norm_matmul(x, attn_norm_g, _layout_w_in(w_in))
    z3 = z.reshape(b, t, Z_WIDTH)

    o_a, conv8, h8 = _lru(z3, conv_buf, h0, lru_conv_w, lru_conv_b, lru_wa, lru_ba, lru_wx, lru_bx,
                          lru_lambda.reshape(-1))
    conv_new, h_new = conv8[:, 8 - (CONV_W - 1):], h8[:, 7]

    wq, wk, wv = _layout_mla(mla_w_q_up, mla_w_kv_up)
    cos64, sin64 = _rope_tables(pos, MLA_ROPE, MLA_ROPE)
    q_b, lat_new, kr_pad = _mla_q(z, cos64, sin64, mla_q_norm_g, mla_kv_norm_g, wq)
    lat_new = lat_new.reshape(b, t, MLA_KV_LORA)
    kr_pad = kr_pad.reshape(b, t, LANES)
    kr_new = kr_pad[:, :, :MLA_ROPE]
    if past_lat is None:
        lat_all, kr_all = lat_new, kr_pad
    else:
        lat_all = jnp.concatenate([past_lat.astype(F32), lat_new], axis=1)
        kr_all = jnp.concatenate([_pad_cols(past_kr.astype(F32), LANES), kr_pad], axis=1)
    lat_all, kr_all = _pad_keys(lat_all, s_pad), _pad_keys(kr_all, s_pad)
    k_b, v_b = _mla_kv(lat_all.reshape(b * s_pad, -1), kr_all.reshape(b * s_pad, -1), wk, wv)
    o_b = _attention(q_b.reshape(b, t, -1), k_b.reshape(b, s_pad, -1), v_b.reshape(b, s_pad, -1), None,
                     groups=MLA_HEADS // 2, hpg=2, dqk=MLA_QK_PAD, kv_shared=False, past=past_len,
                     s_valid=s_valid, tk=tk, scale=(MLA_NOPE + MLA_ROPE) ** -0.5)

    tabs = _rope_tables(pos, DSA_HD, DSA_HD) + _rope_tables(pos, IDX_ROPE, IDX_HD)
    q_c, k_new, iq, ki_pad, iw = _dsa_prep(z, tabs, idx_k_norm_g, idx_k_norm_b)
    v_new = z3[:, :, Z_C_V:Z_C_V + DSA_KV_HEADS * DSA_HD]
    k_new = k_new.reshape(b, t, -1)
    ki_pad = ki_pad.reshape(b, t, LANES)
    ki_new = ki_pad[:, :, :IDX_HD]
    if past_k is None:
        k_all, v_all, ki_all = k_new, v_new, ki_pad
    else:
        k_all = jnp.concatenate([past_k.astype(F32).reshape(b, past_len, -1), k_new], axis=1)
        v_all = jnp.concatenate([past_v.astype(F32).reshape(b, past_len, -1), v_new], axis=1)
        ki_all = jnp.concatenate([_pad_cols(past_ki.astype(F32), LANES), ki_pad], axis=1)
    k_all, v_all, ki_all = (_pad_keys(a.astype(BF16), s_pad) for a in (k_all, v_all, ki_all))
    keep = _dsa_select(iq, iw, ki_all, b=b, t=t, past=past_len, s_valid=s_valid, tk=tk)
    o_c = _attention(q_c.reshape(b, t, -1), k_all, v_all, keep, groups=DSA_KV_HEADS,
                     hpg=DSA_HEADS // DSA_KV_HEADS, dqk=DSA_HD, kv_shared=True, past=past_len, s_valid=s_valid,
                     tk=tk, scale=DSA_HD ** -0.5)

    bf = lambda a: a.astype(BF16)
    x = _merge(x, o_a.reshape(m, -1), o_b.reshape(m, -1), o_c.reshape(m, -1), z, b_gates,
               bf(w_branch_a), bf(w_branch_b), bf(w_branch_c), bf(w_out))
    x = _ffn(x, ffn_norm_g, bf(w_ffn_gate), bf(w_ffn_up), bf(w_ffn_down), final_g)
    new = (lat_new, kr_new, k_new.reshape(b, t, DSA_KV_HEADS, DSA_HD), v_new.reshape(b, t, DSA_KV_HEADS, DSA_HD),
           ki_new, conv_new, h_new)
    return x.reshape(b, t, d), new


def _trunk(x, caches, weights, final_norm_g):
    b, t, _ = x.shape
    depth = weights[0].shape[0]
    past_len = 0 if caches is None else caches[0].shape[2]
    pos = past_len + jnp.arange(t, dtype=I32)
    new = []
    for l in range(depth):
        w_l = tuple(wt[l] for wt in weights)
        if caches is None:
            past = (None, None, None, None, None, jnp.zeros((b, CONV_W - 1, LRU_WIDTH), F32),
                    jnp.zeros((b, LRU_WIDTH), F32))
        else:
            past = tuple(c[l] for c in caches)
        x, st = _layer(x, pos, past, w_l, final_norm_g if l == depth - 1 else None)
        new.append(st)
    return x, tuple(jnp.stack([s[i] for s in new]) for i in range(7))


def kernel(x_prompt, x_sample, cache_mla_latent, cache_mla_krope, cache_dsa_k, cache_dsa_v, cache_dsa_kidx,
           state_lru_conv, state_lru_h, attn_norm_g, w_in, b_gates, lru_conv_w, lru_conv_b, lru_wa, lru_ba,
           lru_wx, lru_bx, lru_lambda, mla_q_norm_g, mla_kv_norm_g, mla_w_q_up, mla_w_kv_up, idx_k_norm_g,
           idx_k_norm_b, w_branch_a, w_branch_b, w_branch_c, w_out, ffn_norm_g, w_ffn_gate, w_ffn_up,
           w_ffn_down, final_norm_g):
    weights = (attn_norm_g, w_in, b_gates, lru_conv_w, lru_conv_b, lru_wa, lru_ba, lru_wx, lru_bx, lru_lambda,
               mla_q_norm_g, mla_kv_norm_g, mla_w_q_up, mla_w_kv_up, idx_k_norm_g, idx_k_norm_b,
               w_branch_a, w_branch_b, w_branch_c, w_out, ffn_norm_g, w_ffn_gate, w_ffn_up, w_ffn_down)
    y_p, st_p = _trunk(x_prompt, None, weights, final_norm_g)
    caches = (cache_mla_latent, cache_mla_krope, cache_dsa_k, cache_dsa_v, cache_dsa_kidx, state_lru_conv,
              state_lru_h)
    y_s, st_s = _trunk(x_sample, caches, weights, final_norm_g)
    (lat_p, kr_p, k_p, v_p, ki_p, conv_p, h_p) = st_p
    (lat_s, kr_s, k_s, v_s, ki_s, conv_s, h_s) = st_s
    return (y_p, y_s, lat_p, lat_s, kr_p, kr_s, k_p, k_s, v_p, v_s, ki_p, ki_s, conv_p, conv_s, h_p, h_s)
```

```python
import functools

import jax
import jax.numpy as jnp
from jax import lax
from jax.experimental import pallas as pl
from jax.experimental.pallas import tpu as pltpu

F32 = jnp.float32
BF16 = jnp.bfloat16
I32 = jnp.int32

D_MODEL = 1024
CHUNK_SHIFT = 6
ROPE_THETA = 10000.0
NORM_EPS = 1e-6
NEG_INF = -1e30
INT_MIN = -(2 ** 31)
LOG2_E = 1.4426950408889634
INDEX_BITS = 14
WORD_KEYS = 256
MASK_DTYPE = jnp.int8

LRU_WIDTH = 1024
LRU_BLOCKS = 8
LRU_BLOCK = 128
CONV_W = 4
LRU_C = 8.0

MLA_HEADS = 8
MLA_Q_LORA = 384
MLA_KV_LORA = 256
MLA_NOPE = 128
MLA_ROPE = 64
MLA_V = 128
MLA_QK_PAD = 256

DSA_HEADS = 8
DSA_KV_HEADS = 4
DSA_HD = 128
IDX_HEADS = 16
IDX_HD = 64
IDX_ROPE = 32
TOPK_MAX = 256
D_FF = 2816

LANES = 128
VMEM_LIMIT = 56 * 1024 * 1024

Z_A_GATE, Z_A_X, Z_C_Q, Z_C_IQ, Z_G = 0, 1024, 2048, 3072, 4096
Z_C_K, Z_C_V, Z_B_KV, Z_B_KR, Z_C_IK, Z_C_IW, Z_B_Q = 7168, 7680, 8192, 8448, 8576, 8704, 8832
Z_WIDTH = 9216


def _cparams(sem):
    return pltpu.CompilerParams(dimension_semantics=sem, vmem_limit_bytes=VMEM_LIMIT)


def _sigmoid(x):
    return 1.0 / (1.0 + jnp.exp(-x))


def _gelu_tanh(x):
    return 0.5 * x * (1.0 + jnp.tanh(0.7978845608028654 * (x + 0.044715 * (x * x * x))))


def _rms(x, g):
    return x * lax.rsqrt(jnp.mean(x * x, axis=-1, keepdims=True) + NORM_EPS) * g


def _row_tile(m, pref):
    t = min(m, pref)
    assert m % t == 0, (m, t)
    return t


def _norm_matmul_kernel(x_ref, g_ref, w_ref, o_ref, xn_ref):
    @pl.when(pl.program_id(1) == 0)
    def _():
        xn_ref[...] = _rms(x_ref[...], g_ref[...]).astype(BF16)

    o_ref[...] = jnp.dot(xn_ref[...], w_ref[...], preferred_element_type=F32)


def _norm_matmul(x, g, w):
    m, d = x.shape
    n = w.shape[1]
    tm, tn = _row_tile(m, 1024), 1024
    return pl.pallas_call(
        _norm_matmul_kernel,
        grid=(m // tm, n // tn),
        in_specs=[pl.BlockSpec((tm, d), lambda i, j: (i, 0)),
                  pl.BlockSpec((1, d), lambda i, j: (0, 0)),
                  pl.BlockSpec((d, tn), lambda i, j: (0, j))],
        out_specs=pl.BlockSpec((tm, tn), lambda i, j: (i, j)),
        out_shape=jax.ShapeDtypeStruct((m, n), F32),
        scratch_shapes=[pltpu.VMEM((tm, d), BF16)],
        compiler_params=_cparams(("parallel", "arbitrary")),
        name="norm_matmul",
    )(x, g.reshape(1, d), w)


def _lru_kernel(gate_ref, xin_ref, cbuf_ref, h0_ref, cw_ref, cb_ref, wa_ref, ba_ref, wx_ref, bx_ref, lam_ref,
                o_ref, clast_ref, hlast_ref, prev_ref, hc_ref, *, tt):
    @pl.when(pl.program_id(1) == 0)
    def _():
        prev_ref[...] = cbuf_ref[...]
        hc_ref[...] = h0_ref[...]

    row = lax.broadcasted_iota(I32, (tt, LRU_BLOCK), 0)
    row8 = lax.broadcasted_iota(I32, (8, LRU_BLOCK), 0)
    for n in range(LRU_BLOCKS):
        sl = slice(n * LRU_BLOCK, (n + 1) * LRU_BLOCK)
        x = xin_ref[:, sl]
        prev = prev_ref[:, sl]
        u = cb_ref[:, sl]
        for j in range(CONV_W):
            d = CONV_W - 1 - j
            if d == 0:
                xs = x
            else:
                rx = pltpu.roll(x, d, axis=0)
                head = jnp.where(row8 < d, pltpu.roll(prev, d, axis=0), rx[:8])
                xs = head if tt == 8 else jnp.concatenate([head, rx[8:]], axis=0)
            u = u + xs * cw_ref[j:j + 1, sl]
        ub = u.astype(BF16)
        r = _sigmoid(jnp.dot(ub, wa_ref[n], preferred_element_type=F32) + ba_ref[:, sl])
        ig = _sigmoid(jnp.dot(ub, wx_ref[n], preferred_element_type=F32) + bx_ref[:, sl])
        lam = lam_ref[:, sl]
        log_sig = jnp.minimum(lam, 0.0) - jnp.log1p(jnp.exp(-jnp.abs(lam)))
        a = jnp.exp(LRU_C * r * log_sig)
        b = jnp.sqrt(1.0 - a * a) * (ig * u)
        d = 1
        while d < tt:
            keep = row >= d
            b = jnp.where(keep, a * pltpu.roll(b, d, axis=0) + b, b)
            a = jnp.where(keep, a * pltpu.roll(a, d, axis=0), a)
            d *= 2
        h = a * hc_ref[:, sl] + b
        hc_ref[:, sl] = h[tt - 1:tt]
        o_ref[:, sl] = (h * _gelu_tanh(gate_ref[:, sl])).astype(o_ref.dtype)
        hlast_ref[:, sl] = h[tt - 8:]
        clast_ref[:, sl] = x[tt - 8:]
        prev_ref[:, sl] = x[tt - 8:]


def _lru(z3, conv_buf, h0, cw, cb, wa, ba, wx, bx, lam):
    b, t, _ = z3.shape
    w = LRU_WIDTH
    tt = _row_tile(t, 256)
    assert tt % 8 == 0 and tt & (tt - 1) == 0
    cbuf8 = jnp.concatenate([jnp.zeros((b, 8 - (CONV_W - 1), w), F32), conv_buf.astype(F32)], axis=1)
    cw8 = jnp.concatenate([cw, jnp.zeros((8 - CONV_W, w), F32)], axis=0)
    row = lambda v: v.reshape(1, w)
    full = lambda shape: pl.BlockSpec(shape, lambda bi, i: (0,) * len(shape))
    return pl.pallas_call(
        functools.partial(_lru_kernel, tt=tt),
        grid=(b, t // tt),
        in_specs=[pl.BlockSpec((None, tt, w), lambda bi, i: (bi, i, Z_A_GATE // w)),
                  pl.BlockSpec((None, tt, w), lambda bi, i: (bi, i, Z_A_X // w)),
                  pl.BlockSpec((None, 8, w), lambda bi, i: (bi, 0, 0)),
                  pl.BlockSpec((None, 1, w), lambda bi, i: (bi, 0, 0)),
                  full((8, w)), full((1, w)),
                  full((LRU_BLOCKS, LRU_BLOCK, LRU_BLOCK)), full((1, w)),
                  full((LRU_BLOCKS, LRU_BLOCK, LRU_BLOCK)), full((1, w)), full((1, w))],
        out_specs=[pl.BlockSpec((None, tt, w), lambda bi, i: (bi, i, 0)),
                   pl.BlockSpec((None, 8, w), lambda bi, i: (bi, 0, 0)),
                   pl.BlockSpec((None, 8, w), lambda bi, i: (bi, 0, 0))],
        out_shape=[jax.ShapeDtypeStruct((b, t, w), BF16),
                   jax.ShapeDtypeStruct((b, 8, w), F32),
                   jax.ShapeDtypeStruct((b, 8, w), F32)],
        scratch_shapes=[pltpu.VMEM((8, w), F32), pltpu.VMEM((1, w), F32)],
        compiler_params=_cparams(("parallel", "arbitrary")),
        name="lru",
    )(z3, z3, cbuf8, h0.astype(F32).reshape(b, 1, w), cw8, row(cb), wa.astype(BF16), row(ba),
      wx.astype(BF16), row(bx), row(lam))


def _rope_tables(pos, d, period, width=LANES):
    half = d // 2
    inv = ROPE_THETA ** (-jnp.arange(0, d, 2, dtype=F32) / d)
    ang = pos.astype(F32)[:, None] * inv[None, :]
    cos, sin = jnp.cos(ang), jnp.sin(ang)
    t = pos.shape[0]
    cos_p = jnp.concatenate([cos, cos, jnp.ones((t, period - d), F32)], axis=1)
    sin_p = jnp.concatenate([-sin, sin, jnp.zeros((t, period - d), F32)], axis=1)
    reps = width // period
    assert half * 2 == d and reps * period == width
    return jnp.tile(cos_p, (1, reps)), jnp.tile(sin_p, (1, reps))


def _rotate(x, cos_t, sin_t, half, period):
    if 2 * half == LANES:
        partner = pltpu.roll(x, half, axis=1)
    else:
        lane = lax.broadcasted_iota(I32, x.shape, 1)
        partner = jnp.where((lane & (period - 1)) < half,
                            pltpu.roll(x, LANES - half, axis=1), pltpu.roll(x, half, axis=1))
    return x * cos_t + partner * sin_t


def _mla_q_kernel(zq_ref, zkv_ref, zkr_ref, cos_ref, sin_ref, qg_ref, kvg_ref, wq_ref, q_ref, lat_ref, kr_ref):
    cos_t, sin_t = cos_ref[...], sin_ref[...]
    qn = _rms(zq_ref[...], qg_ref[...]).astype(BF16)
    q = jnp.dot(qn, wq_ref[...], preferred_element_type=F32)
    for h in range(MLA_HEADS):
        c0 = h * MLA_QK_PAD
        q_ref[:, c0:c0 + LANES] = q[:, c0:c0 + LANES].astype(BF16)
        q_ref[:, c0 + LANES:c0 + 2 * LANES] = _rotate(
            q[:, c0 + LANES:c0 + 2 * LANES], cos_t, sin_t, MLA_ROPE // 2, MLA_ROPE).astype(BF16)
    lat_ref[...] = _rms(zkv_ref[...], kvg_ref[...])
    kr_ref[...] = _rotate(zkr_ref[...], cos_t, sin_t, MLA_ROPE // 2, MLA_ROPE)


def _mla_q(z, cos_t, sin_t, qg, kvg, wq):
    m = z.shape[0]
    t = cos_t.shape[0]
    tm = _row_tile(t, 256)
    nt = t // tm
    zblk = lambda width, off: pl.BlockSpec((tm, width), lambda i: (i, off // width))
    tab = pl.BlockSpec((tm, LANES), lambda i: (i % nt, 0))
    full = lambda shape: pl.BlockSpec(shape, lambda i: (0,) * len(shape))
    return pl.pallas_call(
        _mla_q_kernel,
        grid=(m // tm,),
        in_specs=[zblk(MLA_Q_LORA, Z_B_Q), zblk(MLA_KV_LORA, Z_B_KV), zblk(LANES, Z_B_KR), tab, tab,
                  full((1, MLA_Q_LORA)), full((1, MLA_KV_LORA)), full((MLA_Q_LORA, MLA_HEADS * MLA_QK_PAD))],
        out_specs=[pl.BlockSpec((tm, MLA_HEADS * MLA_QK_PAD), lambda i: (i, 0)),
                   pl.BlockSpec((tm, MLA_KV_LORA), lambda i: (i, 0)),
                   pl.BlockSpec((tm, LANES), lambda i: (i, 0))],
        out_shape=[jax.ShapeDtypeStruct((m, MLA_HEADS * MLA_QK_PAD), BF16),
                   jax.ShapeDtypeStruct((m, MLA_KV_LORA), F32),
                   jax.ShapeDtypeStruct((m, LANES), F32)],
        compiler_params=_cparams(("parallel",)),
        name="mla_q",
    )(z, z, z, cos_t, sin_t, qg.reshape(1, -1), kvg.reshape(1, -1), wq)


def _mla_kv_kernel(lat_ref, kr_ref, wk_ref, wv_ref, k_ref, v_ref):
    latb = lat_ref[...].astype(BF16)
    k = jnp.dot(latb, wk_ref[...], preferred_element_type=F32)
    v_ref[...] = jnp.dot(latb, wv_ref[...], preferred_element_type=F32).astype(BF16)
    krb = kr_ref[...].astype(BF16)
    for h in range(MLA_HEADS):
        k_ref[:, h * MLA_QK_PAD:h * MLA_QK_PAD + LANES] = k[:, h * MLA_NOPE:(h + 1) * MLA_NOPE].astype(BF16)
        k_ref[:, h * MLA_QK_PAD + LANES:(h + 1) * MLA_QK_PAD] = krb


def _mla_kv(lat_all, kr_all, wk, wv):
    m = lat_all.shape[0]
    tm = 384 if m % 512 else 512
    assert m % tm == 0
    full = lambda shape: pl.BlockSpec(shape, lambda i: (0,) * len(shape))
    return pl.pallas_call(
        _mla_kv_kernel,
        grid=(m // tm,),
        in_specs=[pl.BlockSpec((tm, MLA_KV_LORA), lambda i: (i, 0)), pl.BlockSpec((tm, LANES), lambda i: (i, 0)),
                  full(wk.shape), full(wv.shape)],
        out_specs=[pl.BlockSpec((tm, MLA_HEADS * MLA_QK_PAD), lambda i: (i, 0)),
                   pl.BlockSpec((tm, MLA_HEADS * MLA_V), lambda i: (i, 0))],
        out_shape=[jax.ShapeDtypeStruct((m, MLA_HEADS * MLA_QK_PAD), BF16),
                   jax.ShapeDtypeStruct((m, MLA_HEADS * MLA_V), BF16)],
        compiler_params=_cparams(("parallel",)),
        name="mla_kv",
    )(lat_all, kr_all, wk, wv)


def _dsa_prep_kernel(zq_ref, zk_ref, ziq_ref, zik_ref, ziw_ref, cos_ref, sin_ref, cosp_ref, sinp_ref,
                     ikg_ref, ikb_ref, q_ref, k_ref, iq_ref, ki_ref, iw_ref):
    cos_t, sin_t = cos_ref[...], sin_ref[...]
    cos_p, sin_p = cosp_ref[...], sinp_ref[...]
    for h in range(DSA_HEADS):
        sl = slice(h * DSA_HD, (h + 1) * DSA_HD)
        q_ref[:, sl] = _rotate(zq_ref[:, sl], cos_t, sin_t, DSA_HD // 2, DSA_HD).astype(BF16)
    for h in range(DSA_KV_HEADS):
        sl = slice(h * DSA_HD, (h + 1) * DSA_HD)
        k_ref[:, sl] = _rotate(zk_ref[:, sl], cos_t, sin_t, DSA_HD // 2, DSA_HD)
    lane = lax.broadcasted_iota(I32, cos_t.shape, 1)
    low = lane < IDX_HD
    for c in range(IDX_HEADS // 2):
        y = _rotate(ziq_ref[:, c * LANES:(c + 1) * LANES], cos_p, sin_p, IDX_ROPE // 2, IDX_HD)
        iq_ref[2 * c] = jnp.where(low, y, 0.0).astype(BF16)
        iq_ref[2 * c + 1] = jnp.where(low, pltpu.roll(y, IDX_HD, axis=1), 0.0).astype(BF16)
    x = zik_ref[...]
    mean = jnp.sum(x, axis=-1, keepdims=True) * (1.0 / IDX_HD)
    xc = jnp.where(low, x - mean, 0.0)
    var = jnp.sum(xc * xc, axis=-1, keepdims=True) * (1.0 / IDX_HD)
    y = xc * lax.rsqrt(var + NORM_EPS) * ikg_ref[...] + ikb_ref[...]
    ki_ref[...] = _rotate(y, cos_p, sin_p, IDX_ROPE // 2, IDX_HD)
    iw_ref[...] = ziw_ref[...] * ((IDX_HEADS * IDX_HD) ** -0.5)


def _dsa_prep(z, tabs, ikg, ikb):
    m = z.shape[0]
    cos_t, sin_t, cos_p, sin_p = tabs
    t = cos_t.shape[0]
    tm = _row_tile(t, 256)
    nt = t // tm
    zblk = lambda width, off: pl.BlockSpec((tm, width), lambda i: (i, off // width))
    tab = pl.BlockSpec((tm, LANES), lambda i: (i % nt, 0))
    full = lambda shape: pl.BlockSpec(shape, lambda i: (0,) * len(shape))
    pad = lambda v: jnp.concatenate([v, jnp.zeros((LANES - IDX_HD,), F32)]).reshape(1, LANES)
    return pl.pallas_call(
        _dsa_prep_kernel,
        grid=(m // tm,),
        in_specs=[zblk(1024, Z_C_Q), zblk(512, Z_C_K), zblk(1024, Z_C_IQ), zblk(LANES, Z_C_IK), zblk(LANES, Z_C_IW),
                  tab, tab, tab, tab, full((1, LANES)), full((1, LANES))],
        out_specs=[pl.BlockSpec((tm, 1024), lambda i: (i, 0)),
                   pl.BlockSpec((tm, 512), lambda i: (i, 0)),
                   pl.BlockSpec((IDX_HEADS, tm, LANES), lambda i: (0, i, 0)),
                   pl.BlockSpec((tm, LANES), lambda i: (i, 0)),
                   pl.BlockSpec((tm, LANES), lambda i: (i, 0))],
        out_shape=[jax.ShapeDtypeStruct((m, 1024), BF16),
                   jax.ShapeDtypeStruct((m, 512), F32),
                   jax.ShapeDtypeStruct((IDX_HEADS, m, LANES), BF16),
                   jax.ShapeDtypeStruct((m, LANES), F32),
                   jax.ShapeDtypeStruct((m, LANES), F32)],
        compiler_params=_cparams(("parallel",)),
        name="dsa_prep",
    )(z, z, z, z, z, cos_t, sin_t, cos_p, sin_p, pad(ikg), pad(ikb))


def _visible_tiles(i, *, tq, tk, past, s_valid):
    last = past + (i + 1) * tq - 1
    vis_end = jnp.minimum(s_valid, ((last >> CHUNK_SHIFT) + 1) << CHUNK_SHIFT)
    return (vis_end + tk - 1) // tk


def _select_kernel(iq_ref, iw_ref, ki_ref, mask_ref, skey_ref, thr_ref, jcut_ref, *, tq, tk, n_tiles, past,
                   s_valid, topk):
    i = pl.program_id(1)
    nk = _visible_tiles(i, tq=tq, tk=tk, past=past, s_valid=s_valid)
    qchunk = (past + i * tq + lax.broadcasted_iota(I32, (tq, tk), 0)) >> CHUNK_SHIFT
    lane_k = lax.broadcasted_iota(I32, (tq, tk), 1)
    iw = iw_ref[...]
    wcols = [jnp.broadcast_to(iw[:, h:h + 1], (tq, LANES)) for h in range(IDX_HEADS)]

    def score_body(j, carry):
        kt = ki_ref[pl.ds(j * tk, tk), :]
        acc = [jnp.zeros((tq, LANES), F32) for _ in range(tk // LANES)]
        for h in range(IDX_HEADS):
            lg = lax.dot_general(iq_ref[h], kt, (((1,), (1,)), ((), ())), preferred_element_type=F32)
            for c in range(tk // LANES):
                acc[c] = acc[c] + wcols[h] * jnp.maximum(lg[:, c * LANES:(c + 1) * LANES], 0.0)
        score = jnp.concatenate(acc, axis=1)
        bits = lax.bitcast_convert_type(score, I32)
        key = bits ^ ((bits >> 31) & 0x7FFFFFFF)
        kpos = j * tk + lane_k
        vis = ((kpos >> CHUNK_SHIFT) <= qchunk) & (kpos < s_valid)
        skey_ref[:, pl.ds(j * tk, tk)] = jnp.where(vis, key, INT_MIN)
        return carry

    lax.fori_loop(0, nk, score_body, 0)

    rg = min(tq, 128)
    lane_g = lax.broadcasted_iota(I32, (rg, LANES), 1)

    def count(r0, pred):
        def body(j, cnt):
            t = skey_ref[r0:r0 + rg, pl.ds(j * tk, tk)]
            for c in range(tk // LANES):
                cnt = cnt + jnp.where(pred(t[:, c * LANES:(c + 1) * LANES], j * tk + c * LANES), 1, 0)
            return cnt

        cnt = lax.fori_loop(0, nk, body, jnp.zeros((rg, LANES), I32))
        return jnp.sum(cnt, axis=1, keepdims=True)

    def count_ge(r0, cand):
        cand_b = jnp.broadcast_to(cand, (rg, LANES))
        return count(r0, lambda t, k0: t >= cand_b)

    n_ties = jnp.zeros((1, 1), I32)
    for r0 in range(0, tq, rg):
        c0 = count_ge(r0, jnp.zeros((rg, 1), I32))
        nonneg = c0 >= topk
        prefix = jnp.where(nonneg, 0, INT_MIN).astype(I32)
        n_ge = jnp.where(nonneg, c0, nk * tk)

        def bit_body(it, carry, r0=r0):
            prefix, n_ge = carry
            cand = prefix + lax.shift_left(jnp.int32(1), 30 - it)
            c = count_ge(r0, cand)
            ok = c >= topk
            return jnp.where(ok, cand, prefix), jnp.where(ok, c, n_ge)

        prefix, n_ge = lax.fori_loop(0, 31, bit_body, (prefix, n_ge))
        thr_ref[r0:r0 + rg] = jnp.broadcast_to(jnp.maximum(prefix, INT_MIN + 1), (rg, LANES))
        tied = (n_ge > topk) & (prefix > INT_MIN)
        n_ties = n_ties + jnp.sum(jnp.where(tied, 1, 0), axis=0, keepdims=True)
    jcut_ref[...] = jnp.full(jcut_ref.shape, n_tiles * tk, I32)

    @pl.when(n_ties[0, 0] > 0)
    def _():
        for r0 in range(0, tq, rg):
            thr = thr_ref[r0:r0 + rg]
            need = topk - count(r0, lambda t, k0: t > thr)

            def idx_body(it, jcut, r0=r0, thr=thr, need=need):
                cand = jcut + lax.shift_left(jnp.int32(1), INDEX_BITS - 1 - it)
                cand_b = jnp.broadcast_to(cand, (rg, LANES))
                c = count(r0, lambda t, k0: (t == thr) & (k0 + lane_g < cand_b))
                return jnp.where(c < need, cand, jcut)

            jcut = lax.fori_loop(0, INDEX_BITS, idx_body, jnp.zeros((rg, 1), I32))
            jcut_ref[r0:r0 + rg] = jnp.broadcast_to(jcut, (rg, LANES))

    thr_all, jcut_all = thr_ref[...], jcut_ref[...]
    lane_q = lax.broadcasted_iota(I32, (tq, LANES), 1)

    def mask_body(j, carry):
        t = skey_ref[:, pl.ds(j * tk, tk)]
        cols = []
        for c in range(tk // LANES):
            tc = t[:, c * LANES:(c + 1) * LANES]
            kidx = j * tk + c * LANES + lane_q
            cols.append(jnp.where((tc > thr_all) | ((tc == thr_all) & (kidx <= jcut_all)), 1, 0))
        mask_ref[:, pl.ds(j * tk, tk)] = jnp.concatenate(cols, axis=1).astype(mask_ref.dtype)
        return carry

    lax.fori_loop(0, nk, mask_body, 0)

    def zero_body(j, carry):
        mask_ref[:, pl.ds(j * tk, tk)] = jnp.zeros((tq, tk), mask_ref.dtype)
        return carry

    lax.fori_loop(nk, n_tiles, zero_body, 0)


def _bit_transpose32(rows):
    rows = list(rows)
    j, m = 16, 0x0000FFFF
    while j:
        k = 0
        while k < 32:
            t = (rows[k] ^ lax.shift_right_logical(rows[k + j], j)) & _as_i32(m)
            rows[k] = rows[k] ^ t
            rows[k + j] = rows[k + j] ^ lax.shift_left(t, j)
            k = (k + j + 1) & ~j
        j >>= 1
        m = (m ^ (m << j)) & 0xFFFFFFFF
    return rows


def _as_i32(m):
    return m - (1 << 32) if m >= (1 << 31) else m


def _select_cols_kernel(iq_ref, iwt_ref, ki_ref, mask_ref, skey_ref, jcut_ref, planes_ref, alive_ref, ones_ref, *,
                        tq, tk, n_tiles, past, s_valid, topk):
    i = pl.program_id(1)
    nk = _visible_tiles(i, tq=tq, tk=tk, past=past, s_valid=s_valid)
    sub = 32
    gpt = tk // WORD_KEYS
    qchunk = (past + i * tq + lax.broadcasted_iota(I32, (WORD_KEYS, tq), 1)) >> CHUNK_SHIFT
    row_g = lax.broadcasted_iota(I32, (WORD_KEYS, tq), 0)
    row_k = lax.broadcasted_iota(I32, (tk, tq), 0)
    row_s = lax.broadcasted_iota(I32, (sub, tq), 0)

    def score_body(j, carry):
        for g in range(gpt):
            k0 = j * tk + g * WORD_KEYS
            kt = ki_ref[pl.ds(k0, WORD_KEYS), :]
            acc = jnp.zeros((WORD_KEYS, tq), F32)
            for h in range(IDX_HEADS):
                lg = lax.dot_general(kt, iq_ref[h], (((1,), (1,)), ((), ())), preferred_element_type=F32)
                acc = acc + iwt_ref[h:h + 1, :] * jnp.maximum(lg, 0.0)
            bits = lax.bitcast_convert_type(acc, I32)
            ukey = bits ^ ((bits >> 31) | INT_MIN)
            kpos = k0 + row_g
            vis = ((kpos >> CHUNK_SHIFT) <= qchunk) & (kpos < s_valid)
            ukey = jnp.where(vis, ukey, 0)
            skey_ref[pl.ds(k0, WORD_KEYS), :] = ukey ^ INT_MIN
            slabs = _bit_transpose32([ukey[8 * v:8 * v + 8] for v in range(32)])
            for p in range(32):
                planes_ref[j * gpt + g, p] = slabs[p]
            alive_ref[j * gpt + g] = jnp.full((8, tq), -1, I32)
        return carry

    lax.fori_loop(0, nk, score_body, 0)

    def popcount(x):
        x = x - (lax.shift_right_logical(x, 1) & 0x55555555)
        x = (x & 0x33333333) + (lax.shift_right_logical(x, 2) & 0x33333333)
        x = (x + lax.shift_right_logical(x, 4)) & 0x0F0F0F0F
        x = x + lax.shift_right_logical(x, 8)
        return (x + lax.shift_right_logical(x, 16)) & 0x3F

    def bit_body(p, carry):
        thr_u, need, take_prev = carry
        take_b = jnp.broadcast_to(take_prev, (8, tq)) != 0

        def body(j, cnt):
            for g in range(gpt):
                gi = j * gpt + g
                prev_ones = ones_ref[gi]
                alive = jnp.where(take_b, prev_ones, alive_ref[gi] ^ prev_ones)
                ones = alive & planes_ref[gi, p]
                alive_ref[gi] = alive
                ones_ref[gi] = ones
                cnt = cnt + popcount(ones)
            return cnt

        cnt = lax.fori_loop(0, nk, body, jnp.zeros((8, tq), I32))
        c = jnp.sum(cnt, axis=0, keepdims=True)
        take = c >= need
        thr_u = thr_u | jnp.where(take, lax.shift_left(jnp.int32(1), 31 - p), 0)
        return thr_u, jnp.where(take, need, need - c), jnp.where(take, 1, 0)

    def clear_body(j, carry):
        for g in range(gpt):
            ones_ref[j * gpt + g] = jnp.zeros((8, tq), I32)
        return carry

    lax.fori_loop(0, nk, clear_body, 0)
    zero = jnp.zeros((1, tq), I32)
    thr_u, need, take_last = lax.fori_loop(0, 32, bit_body, (zero, zero + topk, zero))
    take_b = jnp.broadcast_to(take_last, (8, tq)) != 0

    def equal_body(j, cnt):
        for g in range(gpt):
            gi = j * gpt + g
            cnt = cnt + popcount(jnp.where(take_b, ones_ref[gi], alive_ref[gi] ^ ones_ref[gi]))
        return cnt

    n_eq = jnp.sum(lax.fori_loop(0, nk, equal_body, jnp.zeros((8, tq), I32)), axis=0, keepdims=True)
    thr = jnp.maximum(thr_u ^ INT_MIN, INT_MIN + 1)
    tied = (n_eq > need) & (thr_u != 0)
    n_ties = jnp.sum(jnp.where(tied, 1, 0), axis=1, keepdims=True)
    jcut_ref[...] = jnp.full(jcut_ref.shape, n_tiles * tk, I32)

    @pl.when(n_ties[0, 0] > 0)
    def _():
        thr_b = jnp.broadcast_to(thr, (sub, tq))

        def count(pred):
            def body(j, cnt):
                for r in range(tk // sub):
                    k0 = j * tk + r * sub
                    cnt = cnt + jnp.where(pred(skey_ref[pl.ds(k0, sub), :], k0), 1, 0)
                return cnt

            cnt = lax.fori_loop(0, nk, body, jnp.zeros((sub, tq), I32))
            return jnp.sum(cnt, axis=0, keepdims=True)

        def idx_body(it, jcut):
            cand_b = jnp.broadcast_to(jcut + lax.shift_left(jnp.int32(1), INDEX_BITS - 1 - it), (sub, tq))
            c = count(lambda t, k0: (t == thr_b) & (k0 + row_s < cand_b))
            return jnp.where(c < need, cand_b[:1], jcut)

        jcut = lax.fori_loop(0, INDEX_BITS, idx_body, jnp.zeros((1, tq), I32))
        jcut_ref[...] = jnp.broadcast_to(jcut, jcut_ref.shape)

    thr_t = jnp.broadcast_to(thr, (tk, tq))
    jcut_t = jnp.broadcast_to(jcut_ref[:1], (tk, tq))

    def mask_body(j, carry):
        t = skey_ref[pl.ds(j * tk, tk), :]
        keep = (t > thr_t) | ((t == thr_t) & (j * tk + row_k <= jcut_t))
        mask_ref[:, pl.ds(j * tk, tk)] = jnp.where(keep, 1, 0).T.astype(mask_ref.dtype)
        return carry

    lax.fori_loop(0, nk, mask_body, 0)

    def zero_body(j, carry):
        mask_ref[:, pl.ds(j * tk, tk)] = jnp.zeros((tq, tk), mask_ref.dtype)
        return carry

    lax.fori_loop(nk, n_tiles, zero_body, 0)


def _dsa_select(iq, iw, ki_all, *, b, t, past, s_valid, tk):
    s_pad = ki_all.shape[1]
    tq = _row_tile(t, 256)
    nq = t // tq
    topk = min(TOPK_MAX, s_valid // 4)
    assert s_pad < 2 ** INDEX_BITS
    if tq % LANES == 0:
        kern = functools.partial(_select_cols_kernel, tq=tq, tk=tk, n_tiles=s_pad // tk, past=past,
                                 s_valid=s_valid, topk=topk)
        iwt = iw[:, :IDX_HEADS].reshape(b, t, IDX_HEADS).transpose(0, 2, 1)
        return pl.pallas_call(
            kern,
            grid=(b, nq),
            in_specs=[pl.BlockSpec((IDX_HEADS, tq, LANES), lambda bi, i: (0, bi * nq + i, 0)),
                      pl.BlockSpec((None, IDX_HEADS, tq), lambda bi, i: (bi, 0, i)),
                      pl.BlockSpec((None, s_pad, LANES), lambda bi, i: (bi, 0, 0))],
            out_specs=pl.BlockSpec((None, tq, s_pad), lambda bi, i: (bi, i, 0)),
            out_shape=jax.ShapeDtypeStruct((b, t, s_pad), MASK_DTYPE),
            scratch_shapes=[pltpu.VMEM((s_pad, tq), I32), pltpu.VMEM((8, tq), I32),
                            pltpu.VMEM((s_pad // WORD_KEYS, 32, 8, tq), I32),
                            pltpu.VMEM((s_pad // WORD_KEYS, 8, tq), I32),
                            pltpu.VMEM((s_pad // WORD_KEYS, 8, tq), I32)],
            compiler_params=_cparams(("parallel", "parallel")),
            name="dsa_select",
        )(iq, iwt, ki_all)
    kern = functools.partial(_select_kernel, tq=tq, tk=tk, n_tiles=s_pad // tk, past=past, s_valid=s_valid,
                             topk=topk)
    return pl.pallas_call(
        kern,
        grid=(b, nq),
        in_specs=[pl.BlockSpec((IDX_HEADS, tq, LANES), lambda bi, i: (0, bi * nq + i, 0)),
                  pl.BlockSpec((tq, LANES), lambda bi, i: (bi * nq + i, 0)),
                  pl.BlockSpec((None, s_pad, LANES), lambda bi, i: (bi, 0, 0))],
        out_specs=pl.BlockSpec((None, tq, s_pad), lambda bi, i: (bi, i, 0)),
        out_shape=jax.ShapeDtypeStruct((b, t, s_pad), MASK_DTYPE),
        scratch_shapes=[pltpu.VMEM((tq, s_pad), I32), pltpu.VMEM((tq, LANES), I32), pltpu.VMEM((tq, LANES), I32)],
        compiler_params=_cparams(("parallel", "parallel")),
        name="dsa_select",
    )(iq, iw, ki_all)


def _attn_kernel(*refs, tq, tk, hpg, dqk, dv, past, s_valid, scale, has_mask, kv_shared):
    if has_mask:
        q_ref, k_ref, v_ref, mask_ref, o_ref, m_ref, acc_ref, sa_ref, sb_ref = refs
    else:
        q_ref, k_ref, v_ref, o_ref, m_ref, acc_ref, sa_ref, sb_ref, vis_ref = refs
    i = pl.program_id(2)
    nk = _visible_tiles(i, tq=tq, tk=tk, past=past, s_valid=s_valid)
    m_ref[...] = jnp.full(m_ref.shape, NEG_INF, F32)
    acc_ref[...] = jnp.zeros(acc_ref.shape, F32)
    lane_k = lax.broadcasted_iota(I32, (tq, tk), 1)
    if not has_mask:
        qchunk = (past + i * tq + lax.broadcasted_iota(I32, (tq, tk), 0)) >> CHUNK_SHIFT
        vis_ref[...] = qchunk - (lane_k >> CHUNK_SHIFT)
    c = scale * LOG2_E
    ones_col = jnp.where(lax.broadcasted_iota(I32, (tk, LANES), 1) == 0, 1.0, 0.0).astype(BF16)

    def scores(j, s_ref):
        rows = pl.ds(j * tk, tk)
        for h in range(hpg):
            hk = 0 if kv_shared else h
            kt = k_ref[rows, hk * dqk:(hk + 1) * dqk]
            q = q_ref[:, h * dqk:(h + 1) * dqk]
            s = lax.dot_general(q, kt, (((1,), (1,)), ((), ())), preferred_element_type=F32)
            s_ref[h] = s * c

    def softmax_pv(j, s_ref):
        rows = pl.ds(j * tk, tk)
        if has_mask:
            keep = mask_ref[:, rows].astype(I32) != 0
        else:
            keep = vis_ref[...] >= ((j * tk) >> CHUNK_SHIFT)
            if s_valid % tk:
                keep = keep & (j * tk + lane_k < s_valid)
        for h in range(hpg):
            hk = 0 if kv_shared else h
            v1 = jnp.concatenate([v_ref[rows, hk * dv:(hk + 1) * dv], ones_col], axis=1)
            s = jnp.where(keep, s_ref[h], NEG_INF)
            m_prev = m_ref[h]
            m_new = jnp.maximum(m_prev, jnp.max(s, axis=1, keepdims=True))
            alpha = jnp.exp2(m_prev - m_new)
            p = jnp.exp2(s - jnp.tile(m_new, (1, tk // LANES)))
            pv = jnp.dot(p.astype(BF16), v1, preferred_element_type=F32)
            acc_ref[h] = jnp.tile(alpha, (1, 2)) * acc_ref[h] + pv
            m_ref[h] = m_new

    def pair_body(jj, carry):
        j0 = 2 * jj
        scores(j0 + 1, sb_ref)
        softmax_pv(j0, sa_ref)
        scores(j0 + 2, sa_ref)
        softmax_pv(j0 + 1, sb_ref)
        return carry

    scores(0, sa_ref)
    n_pairs = (nk - 1) // 2
    lax.fori_loop(0, n_pairs, pair_body, 0)
    j0 = 2 * n_pairs

    @pl.when(nk - j0 == 2)
    def _():
        scores(j0 + 1, sb_ref)
        softmax_pv(j0, sa_ref)
        softmax_pv(j0 + 1, sb_ref)

    @pl.when(nk - j0 == 1)
    def _():
        softmax_pv(j0, sa_ref)

    for h in range(hpg):
        acc = acc_ref[h]
        o_ref[:, h * dv:(h + 1) * dv] = (acc[:, :dv] / acc[:, dv:dv + 1]).astype(o_ref.dtype)


def _attention(q, k, v, mask, *, groups, hpg, dqk, kv_shared, past, s_valid, tk, scale):
    b, t, _ = q.shape
    s_pad = k.shape[1]
    dv = LANES
    kvh = 1 if kv_shared else hpg
    tq = _row_tile(t, 512)
    kern = functools.partial(_attn_kernel, tq=tq, tk=tk, hpg=hpg, dqk=dqk, dv=dv, past=past, s_valid=s_valid,
                             scale=scale, has_mask=mask is not None, kv_shared=kv_shared)
    in_specs = [pl.BlockSpec((None, tq, hpg * dqk), lambda bi, g, i: (bi, i, g)),
                pl.BlockSpec((None, s_pad, kvh * dqk), lambda bi, g, i: (bi, 0, g)),
                pl.BlockSpec((None, s_pad, kvh * dv), lambda bi, g, i: (bi, 0, g))]
    args = [q, k, v]
    if mask is not None:
        in_specs.append(pl.BlockSpec((None, tq, s_pad), lambda bi, g, i: (bi, i, 0)))
        args.append(mask)
    return pl.pallas_call(
        kern,
        grid=(b, groups, t // tq),
        in_specs=in_specs,
        out_specs=pl.BlockSpec((None, tq, hpg * dv), lambda bi, g, i: (bi, i, g)),
        out_shape=jax.ShapeDtypeStruct((b, t, groups * hpg * dv), BF16),
        scratch_shapes=[pltpu.VMEM((hpg, tq, LANES), F32), pltpu.VMEM((hpg, tq, 2 * dv), F32),
                        pltpu.VMEM((hpg, tq, tk), F32), pltpu.VMEM((hpg, tq, tk), F32)]
        + ([] if mask is not None else [pltpu.VMEM((tq, tk), I32)]),
        compiler_params=_cparams(("parallel", "parallel", "arbitrary")),
        name="dsa_attention" if mask is not None else "mla_attention",
    )(*args)


def _merge_kernel(x_ref, oa_ref, ob_ref, oc_ref, g0_ref, g1_ref, g2_ref, bg_ref, wa_ref, wb_ref, wc_ref, wo_ref,
                  o_ref):
    merged = None
    for n, (o_r, g_r, w_r) in enumerate(((oa_ref, g0_ref, wa_ref), (ob_ref, g1_ref, wb_ref),
                                         (oc_ref, g2_ref, wc_ref))):
        gate = _sigmoid(g_r[...] + bg_ref[:, n * D_MODEL:(n + 1) * D_MODEL])
        term = gate * jnp.dot(o_r[...], w_r[...], preferred_element_type=F32)
        merged = term if merged is None else merged + term
    o_ref[...] = x_ref[...] + jnp.dot(merged.astype(BF16), wo_ref[...], preferred_element_type=F32)


def _merge(x, oa, ob, oc, z, bg, wa, wb, wc, wo):
    m, d = x.shape
    tm = _row_tile(m, 512)
    rows = pl.BlockSpec((tm, d), lambda i: (i, 0))
    gblk = lambda n: pl.BlockSpec((tm, d), lambda i: (i, Z_G // d + n))
    full = lambda shape: pl.BlockSpec(shape, lambda i: (0,) * len(shape))
    return pl.pallas_call(
        _merge_kernel,
        grid=(m // tm,),
        in_specs=[rows, rows, rows, rows, gblk(0), gblk(1), gblk(2), full((1, 3 * d)),
                  full((d, d)), full((d, d)), full((d, d)), full((d, d))],
        out_specs=rows,
        out_shape=jax.ShapeDtypeStruct((m, d), F32),
        compiler_params=_cparams(("parallel",)),
        name="merge",
    )(x, oa, ob, oc, z, z, z, bg.reshape(1, -1), wa, wb, wc, wo)


def _ffn_kernel(*refs, final):
    if final:
        x_ref, g_ref, wg_ref, wu_ref, wd_ref, gf_ref, o_ref, hn_ref, acc_ref = refs
    else:
        x_ref, g_ref, wg_ref, wu_ref, wd_ref, o_ref, hn_ref, acc_ref = refs
    j = pl.program_id(1)

    @pl.when(j == 0)
    def _():
        hn_ref[...] = _rms(x_ref[...], g_ref[...]).astype(BF16)
        acc_ref[...] = jnp.zeros(acc_ref.shape, F32)

    hn = hn_ref[...]
    gt = jnp.dot(hn, wg_ref[...], preferred_element_type=F32)
    up = jnp.dot(hn, wu_ref[...], preferred_element_type=F32)
    act = (gt * _sigmoid(gt) * up).astype(BF16)
    acc_ref[...] += jnp.dot(act, wd_ref[...], preferred_element_type=F32)

    @pl.when(j == pl.num_programs(1) - 1)
    def _():
        y = x_ref[...] + acc_ref[...]
        o_ref[...] = _rms(y, gf_ref[...]) if final else y


def _ffn(x, g, wg, wu, wd, gf=None):
    m, d = x.shape
    f = wg.shape[1]
    tm, tf = _row_tile(m, 512), f // 2
    final = gf is not None
    in_specs = [pl.BlockSpec((tm, d), lambda i, j: (i, 0)), pl.BlockSpec((1, d), lambda i, j: (0, 0)),
                pl.BlockSpec((d, tf), lambda i, j: (0, j)), pl.BlockSpec((d, tf), lambda i, j: (0, j)),
                pl.BlockSpec((tf, d), lambda i, j: (j, 0))]
    args = [x, g.reshape(1, d), wg, wu, wd]
    if final:
        in_specs.append(pl.BlockSpec((1, d), lambda i, j: (0, 0)))
        args.append(gf.reshape(1, d))
    return pl.pallas_call(
        functools.partial(_ffn_kernel, final=final),
        grid=(m // tm, f // tf),
        in_specs=in_specs,
        out_specs=pl.BlockSpec((tm, d), lambda i, j: (i, 0)),
        out_shape=jax.ShapeDtypeStruct((m, d), F32),
        scratch_shapes=[pltpu.VMEM((tm, d), BF16), pltpu.VMEM((tm, d), F32)],
        compiler_params=_cparams(("parallel", "arbitrary")),
        name="ffn",
    )(*args)


def _pad_cols(w, width):
    return jnp.concatenate([w, jnp.zeros(w.shape[:-1] + (width - w.shape[-1],), w.dtype)], axis=-1)


def _layout_w_in(w):
    a_gate, a_x, b_q, b_kv, b_kr = w[:, 0:1024], w[:, 1024:2048], w[:, 2048:2432], w[:, 2432:2688], w[:, 2688:2752]
    c_q, c_k, c_v, c_iq = w[:, 2752:3776], w[:, 3776:4288], w[:, 4288:4800], w[:, 4800:5824]
    c_ik, c_iw, g = w[:, 5824:5888], w[:, 5888:5904], w[:, 5904:8976]
    out = jnp.concatenate([a_gate, a_x, c_q, c_iq, g, c_k, c_v, b_kv, _pad_cols(b_kr, LANES),
                           _pad_cols(c_ik, LANES), _pad_cols(c_iw, LANES), b_q], axis=1)
    assert out.shape[1] == Z_WIDTH
    return out.astype(BF16)


def _layout_mla(w_q_up, w_kv_up):
    r = w_q_up.shape[0]
    zeros = jnp.zeros((r, MLA_HEADS, MLA_QK_PAD - MLA_NOPE - MLA_ROPE), w_q_up.dtype)
    wq = jnp.concatenate([w_q_up, zeros], axis=-1).reshape(r, MLA_HEADS * MLA_QK_PAD).astype(BF16)
    wk = w_kv_up[:, :, :MLA_NOPE].reshape(MLA_KV_LORA, MLA_HEADS * MLA_NOPE).astype(BF16)
    wv = w_kv_up[:, :, MLA_NOPE:].reshape(MLA_KV_LORA, MLA_HEADS * MLA_V).astype(BF16)
    return wq, wk, wv


def _key_tile(s_valid):
    return 512 if s_valid % 512 == 0 else 384


def _pad_keys(x, s_pad):
    b, s = x.shape[:2]
    if s == s_pad:
        return x
    return jnp.concatenate([x, jnp.zeros((b, s_pad - s) + x.shape[2:], x.dtype)], axis=1)


def _layer(x3, pos, past, w, final_g):
    (attn_norm_g, w_in, b_gates, lru_conv_w, lru_conv_b, lru_wa, lru_ba, lru_wx, lru_bx, lru_lambda,
     mla_q_norm_g, mla_kv_norm_g, mla_w_q_up, mla_w_kv_up, idx_k_norm_g, idx_k_norm_b,
     w_branch_a, w_branch_b, w_branch_c, w_out, ffn_norm_g, w_ffn_gate, w_ffn_up, w_ffn_down) = w
    past_lat, past_kr, past_k, past_v, past_ki, conv_buf, h0 = past
    b, t, d = x3.shape
    m = b * t
    past_len = 0 if past_lat is None else past_lat.shape[1]
    s_valid = past_len + t
    tk = _key_tile(s_valid)
    s_pad = -(-s_valid // tk) * tk
    x = x3.reshape(m, d)

    z = _norm_matmul(x, attn_norm_g, _layout_w_in(w_in))
    z3 = z.reshape(b, t, Z_WIDTH)

    o_a, conv8, h8 = _lru(z3, conv_buf, h0, lru_conv_w, lru_conv_b, lru_wa, lru_ba, lru_wx, lru_bx,
                          lru_lambda.reshape(-1))
    conv_new, h_new = conv8[:, 8 - (CONV_W - 1):], h8[:, 7]

    wq, wk, wv = _layout_mla(mla_w_q_up, mla_w_kv_up)
    cos64, sin64 = _rope_tables(pos, MLA_ROPE, MLA_ROPE)
    q_b, lat_new, kr_pad = _mla_q(z, cos64, sin64, mla_q_norm_g, mla_kv_norm_g, wq)
    lat_new = lat_new.reshape(b, t, MLA_KV_LORA)
    kr_pad = kr_pad.reshape(b, t, LANES)
    kr_new = kr_pad[:, :, :MLA_ROPE]
    if past_lat is None:
        lat_all, kr_all = lat_new, kr_pad
    else:
        lat_all = jnp.concatenate([past_lat.astype(F32), lat_new], axis=1)
        kr_all = jnp.concatenate([_pad_cols(past_kr.astype(F32), LANES), kr_pad], axis=1)
    lat_all, kr_all = _pad_keys(lat_all, s_pad), _pad_keys(kr_all, s_pad)
    k_b, v_b = _mla_kv(lat_all.reshape(b * s_pad, -1), kr_all.reshape(b * s_pad, -1), wk, wv)
    o_b = _attention(q_b.reshape(b, t, -1), k_b.reshape(b, s_pad, -1), v_b.reshape(b, s_pad, -1), None,
                     groups=MLA_HEADS // 2, hpg=2, dqk=MLA_QK_PAD, kv_shared=False, past=past_len,
                     s_valid=s_valid, tk=tk, scale=(MLA_NOPE + MLA_ROPE) ** -0.5)

    tabs = _rope_tables(pos, DSA_HD, DSA_HD) + _rope_tables(pos, IDX_ROPE, IDX_HD)
    q_c, k_new, iq, ki_pad, iw = _dsa_prep(z, tabs, idx_k_norm_g, idx_k_norm_b)
    v_new = z3[:, :, Z_C_V:Z_C_V + DSA_KV_HEADS * DSA_HD]
    k_new = k_new.reshape(b, t, -1)
    ki_pad = ki_pad.reshape(b, t, LANES)
    ki_new = ki_pad[:, :, :IDX_HD]
    if past_k is None:
        k_all, v_all, ki_all = k_new, v_new, ki_pad
    else:
        k_all = jnp.concatenate([past_k.astype(F32).reshape(b, past_len, -1), k_new], axis=1)
        v_all = jnp.concatenate([past_v.astype(F32).reshape(b, past_len, -1), v_new], axis=1)
        ki_all = jnp.concatenate([_pad_cols(past_ki.astype(F32), LANES), ki_pad], axis=1)
    k_all, v_all, ki_all = (_pad_keys(a.astype(BF16), s_pad) for a in (k_all, v_all, ki_all))
    keep = _dsa_select(iq, iw, ki_all, b=b, t=t, past=past_len, s_valid=s_valid, tk=tk)
    o_c = _attention(q_c.reshape(b, t, -1), k_all, v_all, keep, groups=DSA_KV_HEADS,
                     hpg=DSA_HEADS // DSA_KV_HEADS, dqk=DSA_HD, kv_shared=True, past=past_len, s_valid=s_valid,
                     tk=tk, scale=DSA_HD ** -0.5)

    bf = lambda a: a.astype(BF16)
    x = _merge(x, o_a.reshape(m, -1), o_b.reshape(m, -1), o_c.reshape(m, -1), z, b_gates,
               bf(w_branch_a), bf(w_branch_b), bf(w_branch_c), bf(w_out))
    x = _ffn(x, ffn_norm_g, bf(w_ffn_gate), bf(w_ffn_up), bf(w_ffn_down), final_g)
    new = (lat_new, kr_new, k_new.reshape(b, t, DSA_KV_HEADS, DSA_HD), v_new.reshape(b, t, DSA_KV_HEADS, DSA_HD),
           ki_new, conv_new, h_new)
    return x.reshape(b, t, d), new


def _trunk(x, caches, weights, final_norm_g):
    b, t, _ = x.shape
    depth = weights[0].shape[0]
    past_len = 0 if caches is None else caches[0].shape[2]
    pos = past_len + jnp.arange(t, dtype=I32)
    new = []
    for l in range(depth):
        w_l = tuple(wt[l] for wt in weights)
        if caches is None:
            past = (None, None, None, None, None, jnp.zeros((b, CONV_W - 1, LRU_WIDTH), F32),
                    jnp.zeros((b, LRU_WIDTH), F32))
        else:
            past = tuple(c[l] for c in caches)
        x, st = _layer(x, pos, past, w_l, final_norm_g if l == depth - 1 else None)
        new.append(st)
    return x, tuple(jnp.stack([s[i] for s in new]) for i in range(7))


def kernel(x_prompt, x_sample, cache_mla_latent, cache_mla_krope, cache_dsa_k, cache_dsa_v, cache_dsa_kidx,
           state_lru_conv, state_lru_h, attn_norm_g, w_in, b_gates, lru_conv_w, lru_conv_b, lru_wa, lru_ba,
           lru_wx, lru_bx, lru_lambda, mla_q_norm_g, mla_kv_norm_g, mla_w_q_up, mla_w_kv_up, idx_k_norm_g,
           idx_k_norm_b, w_branch_a, w_branch_b, w_branch_c, w_out, ffn_norm_g, w_ffn_gate, w_ffn_up,
           w_ffn_down, final_norm_g):
    weights = (attn_norm_g, w_in, b_gates, lru_conv_w, lru_conv_b, lru_wa, lru_ba, lru_wx, lru_bx, lru_lambda,
               mla_q_norm_g, mla_kv_norm_g, mla_w_q_up, mla_w_kv_up, idx_k_norm_g, idx_k_norm_b,
               w_branch_a, w_branch_b, w_branch_c, w_out, ffn_norm_g, w_ffn_gate, w_ffn_up, w_ffn_down)
    y_p, st_p = _trunk(x_prompt, None, weights, final_norm_g)
    caches = (cache_mla_latent, cache_mla_krope, cache_dsa_k, cache_dsa_v, cache_dsa_kidx, state_lru_conv,
              state_lru_h)
    y_s, st_s = _trunk(x_sample, caches, weights, final_norm_g)
    (lat_p, kr_p, k_p, v_p, ki_p, conv_p, h_p) = st_p
    (lat_s, kr_s, k_s, v_s, ki_s, conv_s, h_s) = st_s
    return (y_p, y_s, lat_p, lat_s, kr_p, kr_s, k_p, k_s, v_p, v_s, ki_p, ki_s, conv_p, conv_s, h_p, h_s)
```

```python
import functools

import jax
import jax.numpy as jnp
from jax import lax
from jax.experimental import pallas as pl
from jax.experimental.pallas import tpu as pltpu

F32 = jnp.float32
BF16 = jnp.bfloat16
I32 = jnp.int32

D_MODEL = 1024
CHUNK_SHIFT = 6
ROPE_THETA = 10000.0
NORM_EPS = 1e-6
NEG_INF = -1e30
INT_MIN = -(2 ** 31)
LOG2_E = 1.4426950408889634
INDEX_BITS = 14
WORD_KEYS = 256
MASK_DTYPE = jnp.int8

LRU_WIDTH = 1024
LRU_BLOCKS = 8
LRU_BLOCK = 128
CONV_W = 4
LRU_C = 8.0

MLA_HEADS = 8
MLA_Q_LORA = 384
MLA_KV_LORA = 256
MLA_NOPE = 128
MLA_ROPE = 64
MLA_V = 128
MLA_QK_PAD = 256

DSA_HEADS = 8
DSA_KV_HEADS = 4
DSA_HD = 128
IDX_HEADS = 16
IDX_HD = 64
IDX_ROPE = 32
TOPK_MAX = 256
D_FF = 2816

LANES = 128
VMEM_LIMIT = 56 * 1024 * 1024

Z_A_GATE, Z_A_X, Z_C_Q, Z_C_IQ, Z_G = 0, 1024, 2048, 3072, 4096
Z_C_K, Z_C_V, Z_B_KV, Z_B_KR, Z_C_IK, Z_C_IW, Z_B_Q = 7168, 7680, 8192, 8448, 8576, 8704, 8832
Z_WIDTH = 9216


def _cparams(sem):
    return pltpu.CompilerParams(dimension_semantics=sem, vmem_limit_bytes=VMEM_LIMIT)


def _sigmoid(x):
    return 1.0 / (1.0 + jnp.exp(-x))


def _gelu_tanh(x):
    return 0.5 * x * (1.0 + jnp.tanh(0.7978845608028654 * (x + 0.044715 * (x * x * x))))


def _rms(x, g):
    return x * lax.rsqrt(jnp.mean(x * x, axis=-1, keepdims=True) + NORM_EPS) * g


def _row_tile(m, pref):
    t = min(m, pref)
    assert m % t == 0, (m, t)
    return t


def _norm_matmul_kernel(x_ref, g_ref, w_ref, o_ref, xn_ref):
    @pl.when(pl.program_id(1) == 0)
    def _():
        xn_ref[...] = _rms(x_ref[...], g_ref[...]).astype(BF16)

    o_ref[...] = jnp.dot(xn_ref[...], w_ref[...], preferred_element_type=F32)


def _norm_matmul(x, g, w):
    m, d = x.shape
    n = w.shape[1]
    tm, tn = _row_tile(m, 1024), 1024
    return pl.pallas_call(
        _norm_matmul_kernel,
        grid=(m // tm, n // tn),
        in_specs=[pl.BlockSpec((tm, d), lambda i, j: (i, 0)),
                  pl.BlockSpec((1, d), lambda i, j: (0, 0)),
                  pl.BlockSpec((d, tn), lambda i, j: (0, j))],
        out_specs=pl.BlockSpec((tm, tn), lambda i, j: (i, j)),
        out_shape=jax.ShapeDtypeStruct((m, n), F32),
        scratch_shapes=[pltpu.VMEM((tm, d), BF16)],
        compiler_params=_cparams(("parallel", "arbitrary")),
        name="norm_matmul",
    )(x, g.reshape(1, d), w)


def _lru_kernel(gate_ref, xin_ref, cbuf_ref, h0_ref, cw_ref, cb_ref, wa_ref, ba_ref, wx_ref, bx_ref, lam_ref,
                o_ref, clast_ref, hlast_ref, prev_ref, hc_ref, *, tt):
    @pl.when(pl.program_id(1) == 0)
    def _():
        prev_ref[...] = cbuf_ref[...]
        hc_ref[...] = h0_ref[...]

    row = lax.broadcasted_iota(I32, (tt, LRU_BLOCK), 0)
    row8 = lax.broadcasted_iota(I32, (8, LRU_BLOCK), 0)
    for n in range(LRU_BLOCKS):
        sl = slice(n * LRU_BLOCK, (n + 1) * LRU_BLOCK)
        x = xin_ref[:, sl]
        prev = prev_ref[:, sl]
        u = cb_ref[:, sl]
        for j in range(CONV_W):
            d = CONV_W - 1 - j
            if d == 0:
                xs = x
            else:
                rx = pltpu.roll(x, d, axis=0)
                head = jnp.where(row8 < d, pltpu.roll(prev, d, axis=0), rx[:8])
                xs = head if tt == 8 else jnp.concatenate([head, rx[8:]], axis=0)
            u = u + xs * cw_ref[j:j + 1, sl]
        ub = u.astype(BF16)
        r = _sigmoid(jnp.dot(ub, wa_ref[n], preferred_element_type=F32) + ba_ref[:, sl])
        ig = _sigmoid(jnp.dot(ub, wx_ref[n], preferred_element_type=F32) + bx_ref[:, sl])
        lam = lam_ref[:, sl]
        log_sig = jnp.minimum(lam, 0.0) - jnp.log1p(jnp.exp(-jnp.abs(lam)))
        a = jnp.exp(LRU_C * r * log_sig)
        b = jnp.sqrt(1.0 - a * a) * (ig * u)
        d = 1
        while d < tt:
            if d < 8:
                keep = row >= d
                b = jnp.where(keep, a * pltpu.roll(b, d, axis=0) + b, b)
                a = jnp.where(keep, a * pltpu.roll(a, d, axis=0), a)
            else:
                b = jnp.concatenate([b[:d], a[d:] * b[:tt - d] + b[d:]], axis=0)
                a = jnp.concatenate([a[:d], a[d:] * a[:tt - d]], axis=0)
            d *= 2
        h = a * hc_ref[:, sl] + b
        hc_ref[:, sl] = h[tt - 1:tt]
        o_ref[:, sl] = (h * _gelu_tanh(gate_ref[:, sl])).astype(o_ref.dtype)
        hlast_ref[:, sl] = h[tt - 8:]
        clast_ref[:, sl] = x[tt - 8:]
        prev_ref[:, sl] = x[tt - 8:]


def _lru(z3, conv_buf, h0, cw, cb, wa, ba, wx, bx, lam):
    b, t, _ = z3.shape
    w = LRU_WIDTH
    tt = _row_tile(t, 256)
    assert tt % 8 == 0 and tt & (tt - 1) == 0
    cbuf8 = jnp.concatenate([jnp.zeros((b, 8 - (CONV_W - 1), w), F32), conv_buf.astype(F32)], axis=1)
    cw8 = jnp.concatenate([cw, jnp.zeros((8 - CONV_W, w), F32)], axis=0)
    row = lambda v: v.reshape(1, w)
    full = lambda shape: pl.BlockSpec(shape, lambda bi, i: (0,) * len(shape))
    return pl.pallas_call(
        functools.partial(_lru_kernel, tt=tt),
        grid=(b, t // tt),
        in_specs=[pl.BlockSpec((None, tt, w), lambda bi, i: (bi, i, Z_A_GATE // w)),
                  pl.BlockSpec((None, tt, w), lambda bi, i: (bi, i, Z_A_X // w)),
                  pl.BlockSpec((None, 8, w), lambda bi, i: (bi, 0, 0)),
                  pl.BlockSpec((None, 1, w), lambda bi, i: (bi, 0, 0)),
                  full((8, w)), full((1, w)),
                  full((LRU_BLOCKS, LRU_BLOCK, LRU_BLOCK)), full((1, w)),
                  full((LRU_BLOCKS, LRU_BLOCK, LRU_BLOCK)), full((1, w)), full((1, w))],
        out_specs=[pl.BlockSpec((None, tt, w), lambda bi, i: (bi, i, 0)),
                   pl.BlockSpec((None, 8, w), lambda bi, i: (bi, 0, 0)),
                   pl.BlockSpec((None, 8, w), lambda bi, i: (bi, 0, 0))],
        out_shape=[jax.ShapeDtypeStruct((b, t, w), BF16),
                   jax.ShapeDtypeStruct((b, 8, w), F32),
                   jax.ShapeDtypeStruct((b, 8, w), F32)],
        scratch_shapes=[pltpu.VMEM((8, w), F32), pltpu.VMEM((1, w), F32)],
        compiler_params=_cparams(("parallel", "arbitrary")),
        name="lru",
    )(z3, z3, cbuf8, h0.astype(F32).reshape(b, 1, w), cw8, row(cb), wa.astype(BF16), row(ba),
      wx.astype(BF16), row(bx), row(lam))


def _rope_tables(pos, d, period, width=LANES):
    half = d // 2
    inv = ROPE_THETA ** (-jnp.arange(0, d, 2, dtype=F32) / d)
    ang = pos.astype(F32)[:, None] * inv[None, :]
    cos, sin = jnp.cos(ang), jnp.sin(ang)
    t = pos.shape[0]
    cos_p = jnp.concatenate([cos, cos, jnp.ones((t, period - d), F32)], axis=1)
    sin_p = jnp.concatenate([-sin, sin, jnp.zeros((t, period - d), F32)], axis=1)
    reps = width // period
    assert half * 2 == d and reps * period == width
    return jnp.tile(cos_p, (1, reps)), jnp.tile(sin_p, (1, reps))


def _rotate(x, cos_t, sin_t, half, period):
    if 2 * half == LANES:
        partner = pltpu.roll(x, half, axis=1)
    else:
        lane = lax.broadcasted_iota(I32, x.shape, 1)
        partner = jnp.where((lane & (period - 1)) < half,
                            pltpu.roll(x, LANES - half, axis=1), pltpu.roll(x, half, axis=1))
    return x * cos_t + partner * sin_t


def _mla_q_kernel(zq_ref, zkv_ref, zkr_ref, cos_ref, sin_ref, qg_ref, kvg_ref, wq_ref, q_ref, lat_ref, kr_ref):
    cos_t, sin_t = cos_ref[...], sin_ref[...]
    qn = _rms(zq_ref[...], qg_ref[...]).astype(BF16)
    q = jnp.dot(qn, wq_ref[...], preferred_element_type=F32)
    for h in range(MLA_HEADS):
        c0 = h * MLA_QK_PAD
        q_ref[:, c0:c0 + LANES] = q[:, c0:c0 + LANES].astype(BF16)
        q_ref[:, c0 + LANES:c0 + 2 * LANES] = _rotate(
            q[:, c0 + LANES:c0 + 2 * LANES], cos_t, sin_t, MLA_ROPE // 2, MLA_ROPE).astype(BF16)
    lat_ref[...] = _rms(zkv_ref[...], kvg_ref[...])
    kr_ref[...] = _rotate(zkr_ref[...], cos_t, sin_t, MLA_ROPE // 2, MLA_ROPE)


def _mla_q(z, cos_t, sin_t, qg, kvg, wq):
    m = z.shape[0]
    t = cos_t.shape[0]
    tm = _row_tile(t, 256)
    nt = t // tm
    zblk = lambda width, off: pl.BlockSpec((tm, width), lambda i: (i, off // width))
    tab = pl.BlockSpec((tm, LANES), lambda i: (i % nt, 0))
    full = lambda shape: pl.BlockSpec(shape, lambda i: (0,) * len(shape))
    return pl.pallas_call(
        _mla_q_kernel,
        grid=(m // tm,),
        in_specs=[zblk(MLA_Q_LORA, Z_B_Q), zblk(MLA_KV_LORA, Z_B_KV), zblk(LANES, Z_B_KR), tab, tab,
                  full((1, MLA_Q_LORA)), full((1, MLA_KV_LORA)), full((MLA_Q_LORA, MLA_HEADS * MLA_QK_PAD))],
        out_specs=[pl.BlockSpec((tm, MLA_HEADS * MLA_QK_PAD), lambda i: (i, 0)),
                   pl.BlockSpec((tm, MLA_KV_LORA), lambda i: (i, 0)),
                   pl.BlockSpec((tm, LANES), lambda i: (i, 0))],
        out_shape=[jax.ShapeDtypeStruct((m, MLA_HEADS * MLA_QK_PAD), BF16),
                   jax.ShapeDtypeStruct((m, MLA_KV_LORA), F32),
                   jax.ShapeDtypeStruct((m, LANES), F32)],
        compiler_params=_cparams(("parallel",)),
        name="mla_q",
    )(z, z, z, cos_t, sin_t, qg.reshape(1, -1), kvg.reshape(1, -1), wq)


def _mla_kv_kernel(lat_ref, kr_ref, wk_ref, wv_ref, k_ref, v_ref):
    latb = lat_ref[...].astype(BF16)
    k = jnp.dot(latb, wk_ref[...], preferred_element_type=F32)
    v_ref[...] = jnp.dot(latb, wv_ref[...], preferred_element_type=F32).astype(BF16)
    krb = kr_ref[...].astype(BF16)
    for h in range(MLA_HEADS):
        k_ref[:, h * MLA_QK_PAD:h * MLA_QK_PAD + LANES] = k[:, h * MLA_NOPE:(h + 1) * MLA_NOPE].astype(BF16)
        k_ref[:, h * MLA_QK_PAD + LANES:(h + 1) * MLA_QK_PAD] = krb


def _mla_kv(lat_all, kr_all, wk, wv):
    m = lat_all.shape[0]
    tm = 384 if m % 512 else 512
    assert m % tm == 0
    full = lambda shape: pl.BlockSpec(shape, lambda i: (0,) * len(shape))
    return pl.pallas_call(
        _mla_kv_kernel,
        grid=(m // tm,),
        in_specs=[pl.BlockSpec((tm, MLA_KV_LORA), lambda i: (i, 0)), pl.BlockSpec((tm, LANES), lambda i: (i, 0)),
                  full(wk.shape), full(wv.shape)],
        out_specs=[pl.BlockSpec((tm, MLA_HEADS * MLA_QK_PAD), lambda i: (i, 0)),
                   pl.BlockSpec((tm, MLA_HEADS * MLA_V), lambda i: (i, 0))],
        out_shape=[jax.ShapeDtypeStruct((m, MLA_HEADS * MLA_QK_PAD), BF16),
                   jax.ShapeDtypeStruct((m, MLA_HEADS * MLA_V), BF16)],
        compiler_params=_cparams(("parallel",)),
        name="mla_kv",
    )(lat_all, kr_all, wk, wv)


def _dsa_prep_kernel(zq_ref, zk_ref, ziq_ref, zik_ref, ziw_ref, cos_ref, sin_ref, cosp_ref, sinp_ref,
                     ikg_ref, ikb_ref, q_ref, k_ref, iq_ref, ki_ref, iw_ref):
    cos_t, sin_t = cos_ref[...], sin_ref[...]
    cos_p, sin_p = cosp_ref[...], sinp_ref[...]
    for h in range(DSA_HEADS):
        sl = slice(h * DSA_HD, (h + 1) * DSA_HD)
        q_ref[:, sl] = _rotate(zq_ref[:, sl], cos_t, sin_t, DSA_HD // 2, DSA_HD).astype(BF16)
    for h in range(DSA_KV_HEADS):
        sl = slice(h * DSA_HD, (h + 1) * DSA_HD)
        k_ref[:, sl] = _rotate(zk_ref[:, sl], cos_t, sin_t, DSA_HD // 2, DSA_HD)
    lane = lax.broadcasted_iota(I32, cos_t.shape, 1)
    low = lane < IDX_HD
    for c in range(IDX_HEADS // 2):
        y = _rotate(ziq_ref[:, c * LANES:(c + 1) * LANES], cos_p, sin_p, IDX_ROPE // 2, IDX_HD)
        iq_ref[2 * c] = jnp.where(low, y, 0.0).astype(BF16)
        iq_ref[2 * c + 1] = jnp.where(low, pltpu.roll(y, IDX_HD, axis=1), 0.0).astype(BF16)
    x = zik_ref[...]
    mean = jnp.sum(x, axis=-1, keepdims=True) * (1.0 / IDX_HD)
    xc = jnp.where(low, x - mean, 0.0)
    var = jnp.sum(xc * xc, axis=-1, keepdims=True) * (1.0 / IDX_HD)
    y = xc * lax.rsqrt(var + NORM_EPS) * ikg_ref[...] + ikb_ref[...]
    ki_ref[...] = _rotate(y, cos_p, sin_p, IDX_ROPE // 2, IDX_HD)
    iw_ref[...] = ziw_ref[...] * ((IDX_HEADS * IDX_HD) ** -0.5)


def _dsa_prep(z, tabs, ikg, ikb):
    m = z.shape[0]
    cos_t, sin_t, cos_p, sin_p = tabs
    t = cos_t.shape[0]
    tm = _row_tile(t, 256)
    nt = t // tm
    zblk = lambda width, off: pl.BlockSpec((tm, width), lambda i: (i, off // width))
    tab = pl.BlockSpec((tm, LANES), lambda i: (i % nt, 0))
    full = lambda shape: pl.BlockSpec(shape, lambda i: (0,) * len(shape))
    pad = lambda v: jnp.concatenate([v, jnp.zeros((LANES - IDX_HD,), F32)]).reshape(1, LANES)
    return pl.pallas_call(
        _dsa_prep_kernel,
        grid=(m // tm,),
        in_specs=[zblk(1024, Z_C_Q), zblk(512, Z_C_K), zblk(1024, Z_C_IQ), zblk(LANES, Z_C_IK), zblk(LANES, Z_C_IW),
                  tab, tab, tab, tab, full((1, LANES)), full((1, LANES))],
        out_specs=[pl.BlockSpec((tm, 1024), lambda i: (i, 0)),
                   pl.BlockSpec((tm, 512), lambda i: (i, 0)),
                   pl.BlockSpec((IDX_HEADS, tm, LANES), lambda i: (0, i, 0)),
                   pl.BlockSpec((tm, LANES), lambda i: (i, 0)),
                   pl.BlockSpec((tm, LANES), lambda i: (i, 0))],
        out_shape=[jax.ShapeDtypeStruct((m, 1024), BF16),
                   jax.ShapeDtypeStruct((m, 512), F32),
                   jax.ShapeDtypeStruct((IDX_HEADS, m, LANES), BF16),
                   jax.ShapeDtypeStruct((m, LANES), F32),
                   jax.ShapeDtypeStruct((m, LANES), F32)],
        compiler_params=_cparams(("parallel",)),
        name="dsa_prep",
    )(z, z, z, z, z, cos_t, sin_t, cos_p, sin_p, pad(ikg), pad(ikb))


def _visible_tiles(i, *, tq, tk, past, s_valid):
    last = past + (i + 1) * tq - 1
    vis_end = jnp.minimum(s_valid, ((last >> CHUNK_SHIFT) + 1) << CHUNK_SHIFT)
    return (vis_end + tk - 1) // tk


def _select_kernel(iq_ref, iw_ref, ki_ref, mask_ref, skey_ref, thr_ref, jcut_ref, *, tq, tk, n_tiles, past,
                   s_valid, topk):
    i = pl.program_id(1)
    nk = _visible_tiles(i, tq=tq, tk=tk, past=past, s_valid=s_valid)
    qchunk = (past + i * tq + lax.broadcasted_iota(I32, (tq, tk), 0)) >> CHUNK_SHIFT
    lane_k = lax.broadcasted_iota(I32, (tq, tk), 1)
    iw = iw_ref[...]
    wcols = [jnp.broadcast_to(iw[:, h:h + 1], (tq, LANES)) for h in range(IDX_HEADS)]

    def score_body(j, carry):
        kt = ki_ref[pl.ds(j * tk, tk), :]
        acc = [jnp.zeros((tq, LANES), F32) for _ in range(tk // LANES)]
        for h in range(IDX_HEADS):
            lg = lax.dot_general(iq_ref[h], kt, (((1,), (1,)), ((), ())), preferred_element_type=F32)
            for c in range(tk // LANES):
                acc[c] = acc[c] + wcols[h] * jnp.maximum(lg[:, c * LANES:(c + 1) * LANES], 0.0)
        score = jnp.concatenate(acc, axis=1)
        bits = lax.bitcast_convert_type(score, I32)
        key = bits ^ ((bits >> 31) & 0x7FFFFFFF)
        kpos = j * tk + lane_k
        vis = ((kpos >> CHUNK_SHIFT) <= qchunk) & (kpos < s_valid)
        skey_ref[:, pl.ds(j * tk, tk)] = jnp.where(vis, key, INT_MIN)
        return carry

    lax.fori_loop(0, nk, score_body, 0)

    rg = min(tq, 128)
    lane_g = lax.broadcasted_iota(I32, (rg, LANES), 1)

    def count(r0, pred):
        def body(j, cnt):
            t = skey_ref[r0:r0 + rg, pl.ds(j * tk, tk)]
            for c in range(tk // LANES):
                cnt = cnt + jnp.where(pred(t[:, c * LANES:(c + 1) * LANES], j * tk + c * LANES), 1, 0)
            return cnt

        cnt = lax.fori_loop(0, nk, body, jnp.zeros((rg, LANES), I32))
        return jnp.sum(cnt, axis=1, keepdims=True)

    def count_ge(r0, cand):
        cand_b = jnp.broadcast_to(cand, (rg, LANES))
        return count(r0, lambda t, k0: t >= cand_b)

    n_ties = jnp.zeros((1, 1), I32)
    for r0 in range(0, tq, rg):
        c0 = count_ge(r0, jnp.zeros((rg, 1), I32))
        nonneg = c0 >= topk
        prefix = jnp.where(nonneg, 0, INT_MIN).astype(I32)
        n_ge = jnp.where(nonneg, c0, nk * tk)

        def bit_body(it, carry, r0=r0):
            prefix, n_ge = carry
            cand = prefix + lax.shift_left(jnp.int32(1), 30 - it)
            c = count_ge(r0, cand)
            ok = c >= topk
            return jnp.where(ok, cand, prefix), jnp.where(ok, c, n_ge)

        prefix, n_ge = lax.fori_loop(0, 31, bit_body, (prefix, n_ge))
        thr_ref[r0:r0 + rg] = jnp.broadcast_to(jnp.maximum(prefix, INT_MIN + 1), (rg, LANES))
        tied = (n_ge > topk) & (prefix > INT_MIN)
        n_ties = n_ties + jnp.sum(jnp.where(tied, 1, 0), axis=0, keepdims=True)
    jcut_ref[...] = jnp.full(jcut_ref.shape, n_tiles * tk, I32)

    @pl.when(n_ties[0, 0] > 0)
    def _():
        for r0 in range(0, tq, rg):
            thr = thr_ref[r0:r0 + rg]
            need = topk - count(r0, lambda t, k0: t > thr)

            def idx_body(it, jcut, r0=r0, thr=thr, need=need):
                cand = jcut + lax.shift_left(jnp.int32(1), INDEX_BITS - 1 - it)
                cand_b = jnp.broadcast_to(cand, (rg, LANES))
                c = count(r0, lambda t, k0: (t == thr) & (k0 + lane_g < cand_b))
                return jnp.where(c < need, cand, jcut)

            jcut = lax.fori_loop(0, INDEX_BITS, idx_body, jnp.zeros((rg, 1), I32))
            jcut_ref[r0:r0 + rg] = jnp.broadcast_to(jcut, (rg, LANES))

    thr_all, jcut_all = thr_ref[...], jcut_ref[...]
    lane_q = lax.broadcasted_iota(I32, (tq, LANES), 1)

    def mask_body(j, carry):
        t = skey_ref[:, pl.ds(j * tk, tk)]
        cols = []
        for c in range(tk // LANES):
            tc = t[:, c * LANES:(c + 1) * LANES]
            kidx = j * tk + c * LANES + lane_q
            cols.append(jnp.where((tc > thr_all) | ((tc == thr_all) & (kidx <= jcut_all)), 1, 0))
        mask_ref[:, pl.ds(j * tk, tk)] = jnp.concatenate(cols, axis=1).astype(mask_ref.dtype)
        return carry

    lax.fori_loop(0, nk, mask_body, 0)

    def zero_body(j, carry):
        mask_ref[:, pl.ds(j * tk, tk)] = jnp.zeros((tq, tk), mask_ref.dtype)
        return carry

    lax.fori_loop(nk, n_tiles, zero_body, 0)


def _bit_transpose32(rows):
    rows = list(rows)
    j, m = 16, 0x0000FFFF
    while j:
        k = 0
        while k < 32:
            t = (rows[k] ^ lax.shift_right_logical(rows[k + j], j)) & _as_i32(m)
            rows[k] = rows[k] ^ t
            rows[k + j] = rows[k + j] ^ lax.shift_left(t, j)
            k = (k + j + 1) & ~j
        j >>= 1
        m = (m ^ (m << j)) & 0xFFFFFFFF
    return rows


def _as_i32(m):
    return m - (1 << 32) if m >= (1 << 31) else m


def _select_cols_kernel(iq_ref, iwt_ref, ki_ref, mask_ref, skey_ref, jcut_ref, planes_ref, alive_ref, ones_ref, *,
                        tq, tk, n_tiles, past, s_valid, topk):
    i = pl.program_id(1)
    nk = _visible_tiles(i, tq=tq, tk=tk, past=past, s_valid=s_valid)
    sub = 32
    gpt = tk // WORD_KEYS
    qchunk = (past + i * tq + lax.broadcasted_iota(I32, (WORD_KEYS, tq), 1)) >> CHUNK_SHIFT
    row_g = lax.broadcasted_iota(I32, (WORD_KEYS, tq), 0)
    row_k = lax.broadcasted_iota(I32, (tk, tq), 0)
    row_s = lax.broadcasted_iota(I32, (sub, tq), 0)

    def score_body(j, carry, masked):
        for g in range(gpt):
            k0 = j * tk + g * WORD_KEYS
            kt = ki_ref[pl.ds(k0, WORD_KEYS), :]
            acc = jnp.zeros((WORD_KEYS, tq), F32)
            for h in range(IDX_HEADS):
                lg = lax.dot_general(kt, iq_ref[h], (((1,), (1,)), ((), ())), preferred_element_type=F32)
                acc = acc + iwt_ref[h:h + 1, :] * jnp.maximum(lg, 0.0)
            bits = lax.bitcast_convert_type(acc, I32)
            ukey = bits ^ ((bits >> 31) | INT_MIN)
            if masked:
                kpos = k0 + row_g
                vis = ((kpos >> CHUNK_SHIFT) <= qchunk) & (kpos < s_valid)
                ukey = jnp.where(vis, ukey, 0)
            skey_ref[pl.ds(k0, WORD_KEYS), :] = ukey ^ INT_MIN
            slabs = _bit_transpose32([ukey[8 * v:8 * v + 8] for v in range(32)])
            for p in range(32):
                planes_ref[j * gpt + g, p] = slabs[p]
            alive_ref[j * gpt + g] = jnp.full((8, tq), -1, I32)
        return carry

    first_chunk_end = (((past + i * tq) >> CHUNK_SHIFT) + 1) << CHUNK_SHIFT
    n_full = jnp.minimum(jnp.minimum(first_chunk_end // tk, s_valid // tk), nk)
    lax.fori_loop(0, n_full, functools.partial(score_body, masked=False), 0)
    lax.fori_loop(n_full, nk, functools.partial(score_body, masked=True), 0)

    n_pairs = (nk + 1) // 2

    @pl.when(nk < 2 * n_pairs)
    def _():
        for g in range(gpt):
            planes_ref[nk * gpt + g] = jnp.zeros((32, 8, tq), I32)
            alive_ref[nk * gpt + g] = jnp.zeros((8, tq), I32)

    def popcount(x):
        x = x - (lax.shift_right_logical(x, 1) & 0x55555555)
        x = (x & 0x33333333) + (lax.shift_right_logical(x, 2) & 0x33333333)
        x = (x + lax.shift_right_logical(x, 4)) & 0x0F0F0F0F
        x = x + lax.shift_right_logical(x, 8)
        return (x + lax.shift_right_logical(x, 16)) & 0x3F

    def bit_body(p, carry):
        thr_u, need, take_prev = carry
        take_b = jnp.broadcast_to(take_prev, (8, tq)) != 0

        def body(j, cnt):
            for g in range(2 * gpt):
                gi = j * 2 * gpt + g
                prev_ones = ones_ref[gi]
                alive = jnp.where(take_b, prev_ones, alive_ref[gi] ^ prev_ones)
                ones = alive & planes_ref[gi, p]
                alive_ref[gi] = alive
                ones_ref[gi] = ones
                cnt = cnt + popcount(ones)
            return cnt

        cnt = lax.fori_loop(0, n_pairs, body, jnp.zeros((8, tq), I32))
        c = jnp.sum(cnt, axis=0, keepdims=True)
        take = c >= need
        thr_u = thr_u | jnp.where(take, lax.shift_left(jnp.int32(1), 31 - p), 0)
        return thr_u, jnp.where(take, need, need - c), jnp.where(take, 1, 0)

    def clear_body(j, carry):
        for g in range(2 * gpt):
            ones_ref[j * 2 * gpt + g] = jnp.zeros((8, tq), I32)
        return carry

    lax.fori_loop(0, n_pairs, clear_body, 0)
    zero = jnp.zeros((1, tq), I32)
    thr_u, need, take_last = lax.fori_loop(0, 32, bit_body, (zero, zero + topk, zero))
    take_b = jnp.broadcast_to(take_last, (8, tq)) != 0

    def equal_body(j, cnt):
        for g in range(2 * gpt):
            gi = j * 2 * gpt + g
            cnt = cnt + popcount(jnp.where(take_b, ones_ref[gi], alive_ref[gi] ^ ones_ref[gi]))
        return cnt

    n_eq = jnp.sum(lax.fori_loop(0, n_pairs, equal_body, jnp.zeros((8, tq), I32)), axis=0, keepdims=True)
    thr = jnp.maximum(thr_u ^ INT_MIN, INT_MIN + 1)
    tied = (n_eq > need) & (thr_u != 0)
    n_ties = jnp.sum(jnp.where(tied, 1, 0), axis=1, keepdims=True)
    jcut_ref[...] = jnp.full(jcut_ref.shape, n_tiles * tk, I32)

    @pl.when(n_ties[0, 0] > 0)
    def _():
        thr_b = jnp.broadcast_to(thr, (sub, tq))

        def count(pred):
            def body(j, cnt):
                for r in range(tk // sub):
                    k0 = j * tk + r * sub
                    cnt = cnt + jnp.where(pred(skey_ref[pl.ds(k0, sub), :], k0), 1, 0)
                return cnt

            cnt = lax.fori_loop(0, nk, body, jnp.zeros((sub, tq), I32))
            return jnp.sum(cnt, axis=0, keepdims=True)

        def idx_body(it, jcut):
            cand_b = jnp.broadcast_to(jcut + lax.shift_left(jnp.int32(1), INDEX_BITS - 1 - it), (sub, tq))
            c = count(lambda t, k0: (t == thr_b) & (k0 + row_s < cand_b))
            return jnp.where(c < need, cand_b[:1], jcut)

        jcut = lax.fori_loop(0, INDEX_BITS, idx_body, jnp.zeros((1, tq), I32))
        jcut_ref[...] = jnp.broadcast_to(jcut, jcut_ref.shape)

    thr_t = jnp.broadcast_to(thr, (tk, tq))
    jcut_t = jnp.broadcast_to(jcut_ref[:1], (tk, tq))

    def mask_body(j, carry):
        t = skey_ref[pl.ds(j * tk, tk), :]
        keep = (t > thr_t) | ((t == thr_t) & (j * tk + row_k <= jcut_t))
        mask_ref[:, pl.ds(j * tk, tk)] = jnp.where(keep, 1, 0).T.astype(mask_ref.dtype)
        return carry

    lax.fori_loop(0, nk, mask_body, 0)

    def zero_body(j, carry):
        mask_ref[:, pl.ds(j * tk, tk)] = jnp.zeros((tq, tk), mask_ref.dtype)
        return carry

    lax.fori_loop(nk, n_tiles, zero_body, 0)


def _dsa_select(iq, iw, ki_all, *, b, t, past, s_valid, tk):
    s_pad = ki_all.shape[1]
    tq = _row_tile(t, 256)
    nq = t // tq
    topk = min(TOPK_MAX, s_valid // 4)
    assert s_pad < 2 ** INDEX_BITS
    if tq % LANES == 0:
        assert tk % WORD_KEYS == 0 and (s_pad // tk) % 2 == 0
        kern = functools.partial(_select_cols_kernel, tq=tq, tk=tk, n_tiles=s_pad // tk, past=past,
                                 s_valid=s_valid, topk=topk)
        iwt = iw[:, :IDX_HEADS].reshape(b, t, IDX_HEADS).transpose(0, 2, 1)
        return pl.pallas_call(
            kern,
            grid=(b, nq),
            in_specs=[pl.BlockSpec((IDX_HEADS, tq, LANES), lambda bi, i: (0, bi * nq + i, 0)),
                      pl.BlockSpec((None, IDX_HEADS, tq), lambda bi, i: (bi, 0, i)),
                      pl.BlockSpec((None, s_pad, LANES), lambda bi, i: (bi, 0, 0))],
            out_specs=pl.BlockSpec((None, tq, s_pad), lambda bi, i: (bi, i, 0)),
            out_shape=jax.ShapeDtypeStruct((b, t, s_pad), MASK_DTYPE),
            scratch_shapes=[pltpu.VMEM((s_pad, tq), I32), pltpu.VMEM((8, tq), I32),
                            pltpu.VMEM((s_pad // WORD_KEYS, 32, 8, tq), I32),
                            pltpu.VMEM((s_pad // WORD_KEYS, 8, tq), I32),
                            pltpu.VMEM((s_pad // WORD_KEYS, 8, tq), I32)],
            compiler_params=_cparams(("parallel", "parallel")),
            name="dsa_select",
        )(iq, iwt, ki_all)
    kern = functools.partial(_select_kernel, tq=tq, tk=tk, n_tiles=s_pad // tk, past=past, s_valid=s_valid,
                             topk=topk)
    return pl.pallas_call(
        kern,
        grid=(b, nq),
        in_specs=[pl.BlockSpec((IDX_HEADS, tq, LANES), lambda bi, i: (0, bi * nq + i, 0)),
                  pl.BlockSpec((tq, LANES), lambda bi, i: (bi * nq + i, 0)),
                  pl.BlockSpec((None, s_pad, LANES), lambda bi, i: (bi, 0, 0))],
        out_specs=pl.BlockSpec((None, tq, s_pad), lambda bi, i: (bi, i, 0)),
        out_shape=jax.ShapeDtypeStruct((b, t, s_pad), MASK_DTYPE),
        scratch_shapes=[pltpu.VMEM((tq, s_pad), I32), pltpu.VMEM((tq, LANES), I32), pltpu.VMEM((tq, LANES), I32)],
        compiler_params=_cparams(("parallel", "parallel")),
        name="dsa_select",
    )(iq, iw, ki_all)


def _attn_kernel(*refs, tq, tk, hpg, dqk, dv, past, s_valid, scale, has_mask, kv_shared):
    if has_mask:
        q_ref, k_ref, v_ref, mask_ref, o_ref, m_ref, acc_ref, sa_ref, sb_ref = refs
    else:
        q_ref, k_ref, v_ref, o_ref, m_ref, acc_ref, sa_ref, sb_ref, vis_ref = refs
    i = pl.program_id(2)
    nk = _visible_tiles(i, tq=tq, tk=tk, past=past, s_valid=s_valid)
    m_ref[...] = jnp.full(m_ref.shape, NEG_INF, F32)
    acc_ref[...] = jnp.zeros(acc_ref.shape, F32)
    lane_k = lax.broadcasted_iota(I32, (tq, tk), 1)
    if not has_mask:
        qchunk = (past + i * tq + lax.broadcasted_iota(I32, (tq, tk), 0)) >> CHUNK_SHIFT
        vis_ref[...] = qchunk - (lane_k >> CHUNK_SHIFT)
    c = scale * LOG2_E
    ones_col = jnp.where(lax.broadcasted_iota(I32, (tk, LANES), 1) == 0, 1.0, 0.0).astype(BF16)

    def scores(j, s_ref):
        rows = pl.ds(j * tk, tk)
        for h in range(hpg):
            hk = 0 if kv_shared else h
            kt = k_ref[rows, hk * dqk:(hk + 1) * dqk]
            q = q_ref[:, h * dqk:(h + 1) * dqk]
            s = lax.dot_general(q, kt, (((1,), (1,)), ((), ())), preferred_element_type=F32)
            s_ref[h] = s * c

    def softmax_pv(j, s_ref):
        rows = pl.ds(j * tk, tk)
        if has_mask:
            keep = mask_ref[:, rows].astype(I32) != 0
        else:
            keep = vis_ref[...] >= ((j * tk) >> CHUNK_SHIFT)
            if s_valid % tk:
                keep = keep & (j * tk + lane_k < s_valid)
        for h in range(hpg):
            hk = 0 if kv_shared else h
            v1 = jnp.concatenate([v_ref[rows, hk * dv:(hk + 1) * dv], ones_col], axis=1)
            s = jnp.where(keep, s_ref[h], NEG_INF)
            m_prev = m_ref[h]
            m_new = jnp.maximum(m_prev, jnp.max(s, axis=1, keepdims=True))
            alpha = jnp.exp2(m_prev - m_new)
            p = jnp.exp2(s - jnp.tile(m_new, (1, tk // LANES)))
            pv = jnp.dot(p.astype(BF16), v1, preferred_element_type=F32)
            acc_ref[h] = jnp.tile(alpha, (1, 2)) * acc_ref[h] + pv
            m_ref[h] = m_new

    def pair_body(jj, carry):
        j0 = 2 * jj
        scores(j0 + 1, sb_ref)
        softmax_pv(j0, sa_ref)
        scores(j0 + 2, sa_ref)
        softmax_pv(j0 + 1, sb_ref)
        return carry

    scores(0, sa_ref)
    n_pairs = (nk - 1) // 2
    lax.fori_loop(0, n_pairs, pair_body, 0)
    j0 = 2 * n_pairs

    @pl.when(nk - j0 == 2)
    def _():
        scores(j0 + 1, sb_ref)
        softmax_pv(j0, sa_ref)
        softmax_pv(j0 + 1, sb_ref)

    @pl.when(nk - j0 == 1)
    def _():
        softmax_pv(j0, sa_ref)

    for h in range(hpg):
        acc = acc_ref[h]
        o_ref[:, h * dv:(h + 1) * dv] = (acc[:, :dv] / acc[:, dv:dv + 1]).astype(o_ref.dtype)


def _attention(q, k, v, mask, *, groups, hpg, dqk, kv_shared, past, s_valid, tk, scale):
    b, t, _ = q.shape
    s_pad = k.shape[1]
    dv = LANES
    kvh = 1 if kv_shared else hpg
    tq = _row_tile(t, 1024)
    kern = functools.partial(_attn_kernel, tq=tq, tk=tk, hpg=hpg, dqk=dqk, dv=dv, past=past, s_valid=s_valid,
                             scale=scale, has_mask=mask is not None, kv_shared=kv_shared)
    in_specs = [pl.BlockSpec((None, tq, hpg * dqk), lambda bi, g, i: (bi, i, g)),
                pl.BlockSpec((None, s_pad, kvh * dqk), lambda bi, g, i: (bi, 0, g)),
                pl.BlockSpec((None, s_pad, kvh * dv), lambda bi, g, i: (bi, 0, g))]
    args = [q, k, v]
    if mask is not None:
        in_specs.append(pl.BlockSpec((None, tq, s_pad), lambda bi, g, i: (bi, i, 0)))
        args.append(mask)
    return pl.pallas_call(
        kern,
        grid=(b, groups, t // tq),
        in_specs=in_specs,
        out_specs=pl.BlockSpec((None, tq, hpg * dv), lambda bi, g, i: (bi, i, g)),
        out_shape=jax.ShapeDtypeStruct((b, t, groups * hpg * dv), BF16),
        scratch_shapes=[pltpu.VMEM((hpg, tq, LANES), F32), pltpu.VMEM((hpg, tq, 2 * dv), F32),
                        pltpu.VMEM((hpg, tq, tk), F32), pltpu.VMEM((hpg, tq, tk), F32)]
        + ([] if mask is not None else [pltpu.VMEM((tq, tk), I32)]),
        compiler_params=_cparams(("parallel", "parallel", "arbitrary")),
        name="dsa_attention" if mask is not None else "mla_attention",
    )(*args)


def _merge_kernel(x_ref, oa_ref, ob_ref, oc_ref, g0_ref, g1_ref, g2_ref, bg_ref, wa_ref, wb_ref, wc_ref, wo_ref,
                  o_ref):
    merged = None
    for n, (o_r, g_r, w_r) in enumerate(((oa_ref, g0_ref, wa_ref), (ob_ref, g1_ref, wb_ref),
                                         (oc_ref, g2_ref, wc_ref))):
        gate = _sigmoid(g_r[...] + bg_ref[:, n * D_MODEL:(n + 1) * D_MODEL])
        term = gate * jnp.dot(o_r[...], w_r[...], preferred_element_type=F32)
        merged = term if merged is None else merged + term
    o_ref[...] = x_ref[...] + jnp.dot(merged.astype(BF16), wo_ref[...], preferred_element_type=F32)


def _merge(x, oa, ob, oc, z, bg, wa, wb, wc, wo):
    m, d = x.shape
    tm = _row_tile(m, 512)
    rows = pl.BlockSpec((tm, d), lambda i: (i, 0))
    gblk = lambda n: pl.BlockSpec((tm, d), lambda i: (i, Z_G // d + n))
    full = lambda shape: pl.BlockSpec(shape, lambda i: (0,) * len(shape))
    return pl.pallas_call(
        _merge_kernel,
        grid=(m // tm,),
        in_specs=[rows, rows, rows, rows, gblk(0), gblk(1), gblk(2), full((1, 3 * d)),
                  full((d, d)), full((d, d)), full((d, d)), full((d, d))],
        out_specs=rows,
        out_shape=jax.ShapeDtypeStruct((m, d), F32),
        compiler_params=_cparams(("parallel",)),
        name="merge",
    )(x, oa, ob, oc, z, z, z, bg.reshape(1, -1), wa, wb, wc, wo)


def _ffn_kernel(*refs, final):
    if final:
        x_ref, g_ref, wg_ref, wu_ref, wd_ref, gf_ref, o_ref, hn_ref, acc_ref = refs
    else:
        x_ref, g_ref, wg_ref, wu_ref, wd_ref, o_ref, hn_ref, acc_ref = refs
    j = pl.program_id(1)

    @pl.when(j == 0)
    def _():
        hn_ref[...] = _rms(x_ref[...], g_ref[...]).astype(BF16)
        acc_ref[...] = jnp.zeros(acc_ref.shape, F32)

    hn = hn_ref[...]
    gt = jnp.dot(hn, wg_ref[...], preferred_element_type=F32)
    up = jnp.dot(hn, wu_ref[...], preferred_element_type=F32)
    act = (gt * _sigmoid(gt) * up).astype(BF16)
    acc_ref[...] += jnp.dot(act, wd_ref[...], preferred_element_type=F32)

    @pl.when(j == pl.num_programs(1) - 1)
    def _():
        y = x_ref[...] + acc_ref[...]
        o_ref[...] = _rms(y, gf_ref[...]) if final else y


def _ffn(x, g, wg, wu, wd, gf=None):
    m, d = x.shape
    f = wg.shape[1]
    tm, tf = _row_tile(m, 512), f // 2
    final = gf is not None
    in_specs = [pl.BlockSpec((tm, d), lambda i, j: (i, 0)), pl.BlockSpec((1, d), lambda i, j: (0, 0)),
                pl.BlockSpec((d, tf), lambda i, j: (0, j)), pl.BlockSpec((d, tf), lambda i, j: (0, j)),
                pl.BlockSpec((tf, d), lambda i, j: (j, 0))]
    args = [x, g.reshape(1, d), wg, wu, wd]
    if final:
        in_specs.append(pl.BlockSpec((1, d), lambda i, j: (0, 0)))
        args.append(gf.reshape(1, d))
    return pl.pallas_call(
        functools.partial(_ffn_kernel, final=final),
        grid=(m // tm, f // tf),
        in_specs=in_specs,
        out_specs=pl.BlockSpec((tm, d), lambda i, j: (i, 0)),
        out_shape=jax.ShapeDtypeStruct((m, d), F32),
        scratch_shapes=[pltpu.VMEM((tm, d), BF16), pltpu.VMEM((tm, d), F32)],
        compiler_params=_cparams(("parallel", "arbitrary")),
        name="ffn",
    )(*args)


def _pad_cols(w, width):
    return jnp.concatenate([w, jnp.zeros(w.shape[:-1] + (width - w.shape[-1],), w.dtype)], axis=-1)


def _layout_w_in(w):
    a_gate, a_x, b_q, b_kv, b_kr = w[:, 0:1024], w[:, 1024:2048], w[:, 2048:2432], w[:, 2432:2688], w[:, 2688:2752]
    c_q, c_k, c_v, c_iq = w[:, 2752:3776], w[:, 3776:4288], w[:, 4288:4800], w[:, 4800:5824]
    c_ik, c_iw, g = w[:, 5824:5888], w[:, 5888:5904], w[:, 5904:8976]
    out = jnp.concatenate([a_gate, a_x, c_q, c_iq, g, c_k, c_v, b_kv, _pad_cols(b_kr, LANES),
                           _pad_cols(c_ik, LANES), _pad_cols(c_iw, LANES), b_q], axis=1)
    assert out.shape[1] == Z_WIDTH
    return out.astype(BF16)


def _layout_mla(w_q_up, w_kv_up):
    r = w_q_up.shape[0]
    zeros = jnp.zeros((r, MLA_HEADS, MLA_QK_PAD - MLA_NOPE - MLA_ROPE), w_q_up.dtype)
    wq = jnp.concatenate([w_q_up, zeros], axis=-1).reshape(r, MLA_HEADS * MLA_QK_PAD).astype(BF16)
    wk = w_kv_up[:, :, :MLA_NOPE].reshape(MLA_KV_LORA, MLA_HEADS * MLA_NOPE).astype(BF16)
    wv = w_kv_up[:, :, MLA_NOPE:].reshape(MLA_KV_LORA, MLA_HEADS * MLA_V).astype(BF16)
    return wq, wk, wv


def _key_tile(s_valid):
    return 512 if s_valid % 512 == 0 else 384


def _pad_keys(x, s_pad):
    b, s = x.shape[:2]
    if s == s_pad:
        return x
    return jnp.concatenate([x, jnp.zeros((b, s_pad - s) + x.shape[2:], x.dtype)], axis=1)


def _layer(x3, pos, past, w, final_g):
    (attn_norm_g, w_in, b_gates, lru_conv_w, lru_conv_b, lru_wa, lru_ba, lru_wx, lru_bx, lru_lambda,
     mla_q_norm_g, mla_kv_norm_g, mla_w_q_up, mla_w_kv_up, idx_k_norm_g, idx_k_norm_b,
     w_branch_a, w_branch_b, w_branch_c, w_out, ffn_norm_g, w_ffn_gate, w_ffn_up, w_ffn_down) = w
    past_lat, past_kr, past_k, past_v, past_ki, conv_buf, h0 = past
    b, t, d = x3.shape
    m = b * t
    past_len = 0 if past_lat is None else past_lat.shape[1]
    s_valid = past_len + t
    tk = _key_tile(s_valid)
    s_pad = -(-s_valid // tk) * tk
    x = x3.reshape(m, d)

    z = _norm_matmul(x, attn_norm_g, _layout_w_in(w_in))
    z3 = z.reshape(b, t, Z_WIDTH)

    o_a, conv8, h8 = _lru(z3, conv_buf, h0, lru_conv_w, lru_conv_b, lru_wa, lru_ba, lru_wx, lru_bx,
                          lru_lambda.reshape(-1))
    conv_new, h_new = conv8[:, 8 - (CONV_W - 1):], h8[:, 7]

    wq, wk, wv = _layout_mla(mla_w_q_up, mla_w_kv_up)
    cos64, sin64 = _rope_tables(pos, MLA_ROPE, MLA_ROPE)
    q_b, lat_new, kr_pad = _mla_q(z, cos64, sin64, mla_q_norm_g, mla_kv_norm_g, wq)
    lat_new = lat_new.reshape(b, t, MLA_KV_LORA)
    kr_pad = kr_pad.reshape(b, t, LANES)
    kr_new = kr_pad[:, :, :MLA_ROPE]
    if past_lat is None:
        lat_all, kr_all = lat_new, kr_pad
    else:
        lat_all = jnp.concatenate([past_lat.astype(F32), lat_new], axis=1)
        kr_all = jnp.concatenate([_pad_cols(past_kr.astype(F32), LANES), kr_pad], axis=1)
    lat_all, kr_all = _pad_keys(lat_all, s_pad), _pad_keys(kr_all, s_pad)
    k_b, v_b = _mla_kv(lat_all.reshape(b * s_pad, -1), kr_all.reshape(b * s_pad, -1), wk, wv)
    o_b = _attention(q_b.reshape(b, t, -1), k_b.reshape(b, s_pad, -1), v_b.reshape(b, s_pad, -1), None,
                     groups=MLA_HEADS // 2, hpg=2, dqk=MLA_QK_PAD, kv_shared=False, past=past_len,
                     s_valid=s_valid, tk=tk, scale=(MLA_NOPE + MLA_ROPE) ** -0.5)

    tabs = _rope_tables(pos, DSA_HD, DSA_HD) + _rope_tables(pos, IDX_ROPE, IDX_HD)
    q_c, k_new, iq, ki_pad, iw = _dsa_prep(z, tabs, idx_k_norm_g, idx_k_norm_b)
    v_new = z3[:, :, Z_C_V:Z_C_V + DSA_KV_HEADS * DSA_HD]
    k_new = k_new.reshape(b, t, -1)
    ki_pad = ki_pad.reshape(b, t, LANES)
    ki_new = ki_pad[:, :, :IDX_HD]
    if past_k is None:
        k_all, v_all, ki_all = k_new, v_new, ki_pad
    else:
        k_all = jnp.concatenate([past_k.astype(F32).reshape(b, past_len, -1), k_new], axis=1)
        v_all = jnp.concatenate([past_v.astype(F32).reshape(b, past_len, -1), v_new], axis=1)
        ki_all = jnp.concatenate([_pad_cols(past_ki.astype(F32), LANES), ki_pad], axis=1)
    k_all, v_all, ki_all = (_pad_keys(a.astype(BF16), s_pad) for a in (k_all, v_all, ki_all))
    keep = _dsa_select(iq, iw, ki_all, b=b, t=t, past=past_len, s_valid=s_valid, tk=tk)
    o_c = _attention(q_c.reshape(b, t, -1), k_all, v_all, keep, groups=DSA_KV_HEADS,
                     hpg=DSA_HEADS // DSA_KV_HEADS, dqk=DSA_HD, kv_shared=True, past=past_len, s_valid=s_valid,
                     tk=tk, scale=DSA_HD ** -0.5)

    bf = lambda a: a.astype(BF16)
    x = _merge(x, o_a.reshape(m, -1), o_b.reshape(m, -1), o_c.reshape(m, -1), z, b_gates,
               bf(w_branch_a), bf(w_branch_b), bf(w_branch_c), bf(w_out))
    x = _ffn(x, ffn_norm_g, bf(w_ffn_gate), bf(w_ffn_up), bf(w_ffn_down), final_g)
    new = (lat_new, kr_new, k_new.reshape(b, t, DSA_KV_HEADS, DSA_HD), v_new.reshape(b, t, DSA_KV_HEADS, DSA_HD),
           ki_new, conv_new, h_new)
    return x.reshape(b, t, d), new


def _trunk(x, caches, weights, final_norm_g):
    b, t, _ = x.shape
    depth = weights[0].shape[0]
    past_len = 0 if caches is None else caches[0].shape[2]
    pos = past_len + jnp.arange(t, dtype=I32)
    new = []
    for l in range(depth):
        w_l = tuple(wt[l] for wt in weights)
        if caches is None:
            past = (None, None, None, None, None, jnp.zeros((b, CONV_W - 1, LRU_WIDTH), F32),
                    jnp.zeros((b, LRU_WIDTH), F32))
        else:
            past = tuple(c[l] for c in caches)
        x, st = _layer(x, pos, past, w_l, final_norm_g if l == depth - 1 else None)
        new.append(st)
    return x, tuple(jnp.stack([s[i] for s in new]) for i in range(7))


def kernel(x_prompt, x_sample, cache_mla_latent, cache_mla_krope, cache_dsa_k, cache_dsa_v, cache_dsa_kidx,
           state_lru_conv, state_lru_h, attn_norm_g, w_in, b_gates, lru_conv_w, lru_conv_b, lru_wa, lru_ba,
           lru_wx, lru_bx, lru_lambda, mla_q_norm_g, mla_kv_norm_g, mla_w_q_up, mla_w_kv_up, idx_k_norm_g,
           idx_k_norm_b, w_branch_a, w_branch_b, w_branch_c, w_out, ffn_norm_g, w_ffn_gate, w_ffn_up,
           w_ffn_down, final_norm_g):
    weights = (attn_norm_g, w_in, b_gates, lru_conv_w, lru_conv_b, lru_wa, lru_ba, lru_wx, lru_bx, lru_lambda,
               mla_q_norm_g, mla_kv_norm_g, mla_w_q_up, mla_w_kv_up, idx_k_norm_g, idx_k_norm_b,
               w_branch_a, w_branch_b, w_branch_c, w_out, ffn_norm_g, w_ffn_gate, w_ffn_up, w_ffn_down)
    y_p, st_p = _trunk(x_prompt, None, weights, final_norm_g)
    caches = (cache_mla_latent, cache_mla_krope, cache_dsa_k, cache_dsa_v, cache_dsa_kidx, state_lru_conv,
              state_lru_h)
    y_s, st_s = _trunk(x_sample, caches, weights, final_norm_g)
    (lat_p, kr_p, k_p, v_p, ki_p, conv_p, h_p) = st_p
    (lat_s, kr_s, k_s, v_s, ki_s, conv_s, h_s) = st_s
    return (y_p, y_s, lat_p, lat_s, kr_p, kr_s, k_p, k_s, v_p, v_s, ki_p, ki_s, conv_p, conv_s, h_p, h_s)
```

```python
import functools

import jax
import jax.numpy as jnp
from jax import lax
from jax.experimental import pallas as pl
from jax.experimental.pallas import tpu as pltpu

F32 = jnp.float32
BF16 = jnp.bfloat16
I32 = jnp.int32

D_MODEL = 1024
CHUNK_SHIFT = 6
ROPE_THETA = 10000.0
NORM_EPS = 1e-6
NEG_INF = -1e30
INT_MIN = -(2 ** 31)
LOG2_E = 1.4426950408889634
INDEX_BITS = 14
WORD_KEYS = 256
MASK_DTYPE = jnp.int8

LRU_WIDTH = 1024
LRU_BLOCKS = 8
LRU_BLOCK = 128
CONV_W = 4
LRU_C = 8.0

MLA_HEADS = 8
MLA_Q_LORA = 384
MLA_KV_LORA = 256
MLA_NOPE = 128
MLA_ROPE = 64
MLA_V = 128
MLA_QK_PAD = 256

DSA_HEADS = 8
DSA_KV_HEADS = 4
DSA_HD = 128
IDX_HEADS = 16
IDX_HD = 64
IDX_ROPE = 32
TOPK_MAX = 256
D_FF = 2816

LANES = 128
VMEM_LIMIT = 56 * 1024 * 1024

Z_A_GATE, Z_A_X, Z_C_Q, Z_C_IQ, Z_G = 0, 1024, 2048, 3072, 4096
Z_C_K, Z_C_V, Z_B_KV, Z_B_KR, Z_C_IK, Z_C_IW, Z_B_Q = 7168, 7680, 8192, 8448, 8576, 8704, 8832
Z_WIDTH = 9216


def _cparams(sem):
    return pltpu.CompilerParams(dimension_semantics=sem, vmem_limit_bytes=VMEM_LIMIT)


def _sigmoid(x):
    return 1.0 / (1.0 + jnp.exp(-x))


def _gelu_tanh(x):
    return 0.5 * x * (1.0 + jnp.tanh(0.7978845608028654 * (x + 0.044715 * (x * x * x))))


def _rms(x, g):
    return x * lax.rsqrt(jnp.mean(x * x, axis=-1, keepdims=True) + NORM_EPS) * g


def _row_tile(m, pref):
    t = min(m, pref)
    assert m % t == 0, (m, t)
    return t


def _norm_matmul_kernel(x_ref, g_ref, w_ref, o_ref, xn_ref):
    @pl.when(pl.program_id(1) == 0)
    def _():
        xn_ref[...] = _rms(x_ref[...], g_ref[...]).astype(BF16)

    o_ref[...] = jnp.dot(xn_ref[...], w_ref[...], preferred_element_type=F32)


def _norm_matmul(x, g, w):
    m, d = x.shape
    n = w.shape[1]
    tm, tn = _row_tile(m, 1024), 1024
    return pl.pallas_call(
        _norm_matmul_kernel,
        grid=(m // tm, n // tn),
        in_specs=[pl.BlockSpec((tm, d), lambda i, j: (i, 0)),
                  pl.BlockSpec((1, d), lambda i, j: (0, 0)),
                  pl.BlockSpec((d, tn), lambda i, j: (0, j))],
        out_specs=pl.BlockSpec((tm, tn), lambda i, j: (i, j)),
        out_shape=jax.ShapeDtypeStruct((m, n), F32),
        scratch_shapes=[pltpu.VMEM((tm, d), BF16)],
        compiler_params=_cparams(("parallel", "arbitrary")),
        name="norm_matmul",
    )(x, g.reshape(1, d), w)


def _lru_kernel(gate_ref, xin_ref, cbuf_ref, h0_ref, cw_ref, cb_ref, wa_ref, ba_ref, wx_ref, bx_ref, lam_ref,
                o_ref, clast_ref, hlast_ref, prev_ref, hc_ref, *, tt):
    @pl.when(pl.program_id(1) == 0)
    def _():
        prev_ref[...] = cbuf_ref[...]
        hc_ref[...] = h0_ref[...]

    row = lax.broadcasted_iota(I32, (tt, LRU_BLOCK), 0)
    row8 = lax.broadcasted_iota(I32, (8, LRU_BLOCK), 0)
    for n in range(LRU_BLOCKS):
        sl = slice(n * LRU_BLOCK, (n + 1) * LRU_BLOCK)
        x = xin_ref[:, sl]
        prev = prev_ref[:, sl]
        u = cb_ref[:, sl]
        for j in range(CONV_W):
            d = CONV_W - 1 - j
            if d == 0:
                xs = x
            else:
                rx = pltpu.roll(x, d, axis=0)
                head = jnp.where(row8 < d, pltpu.roll(prev, d, axis=0), rx[:8])
                xs = head if tt == 8 else jnp.concatenate([head, rx[8:]], axis=0)
            u = u + xs * cw_ref[j:j + 1, sl]
        ub = u.astype(BF16)
        r = _sigmoid(jnp.dot(ub, wa_ref[n], preferred_element_type=F32) + ba_ref[:, sl])
        ig = _sigmoid(jnp.dot(ub, wx_ref[n], preferred_element_type=F32) + bx_ref[:, sl])
        lam = lam_ref[:, sl]
        log_sig = jnp.minimum(lam, 0.0) - jnp.log1p(jnp.exp(-jnp.abs(lam)))
        a = jnp.exp(LRU_C * r * log_sig)
        b = jnp.sqrt(1.0 - a * a) * (ig * u)
        d = 1
        while d < tt:
            if d < 8:
                keep = row >= d
                b = jnp.where(keep, a * pltpu.roll(b, d, axis=0) + b, b)
                a = jnp.where(keep, a * pltpu.roll(a, d, axis=0), a)
            else:
                b = jnp.concatenate([b[:d], a[d:] * b[:tt - d] + b[d:]], axis=0)
                a = jnp.concatenate([a[:d], a[d:] * a[:tt - d]], axis=0)
            d *= 2
        h = a * hc_ref[:, sl] + b
        hc_ref[:, sl] = h[tt - 1:tt]
        o_ref[:, sl] = (h * _gelu_tanh(gate_ref[:, sl])).astype(o_ref.dtype)
        hlast_ref[:, sl] = h[tt - 8:]
        clast_ref[:, sl] = x[tt - 8:]
        prev_ref[:, sl] = x[tt - 8:]


def _lru(z3, conv_buf, h0, cw, cb, wa, ba, wx, bx, lam):
    b, t, _ = z3.shape
    w = LRU_WIDTH
    tt = _row_tile(t, 256)
    assert tt % 8 == 0 and tt & (tt - 1) == 0
    cbuf8 = jnp.concatenate([jnp.zeros((b, 8 - (CONV_W - 1), w), F32), conv_buf.astype(F32)], axis=1)
    cw8 = jnp.concatenate([cw, jnp.zeros((8 - CONV_W, w), F32)], axis=0)
    row = lambda v: v.reshape(1, w)
    full = lambda shape: pl.BlockSpec(shape, lambda bi, i: (0,) * len(shape))
    return pl.pallas_call(
        functools.partial(_lru_kernel, tt=tt),
        grid=(b, t // tt),
        in_specs=[pl.BlockSpec((None, tt, w), lambda bi, i: (bi, i, Z_A_GATE // w)),
                  pl.BlockSpec((None, tt, w), lambda bi, i: (bi, i, Z_A_X // w)),
                  pl.BlockSpec((None, 8, w), lambda bi, i: (bi, 0, 0)),
                  pl.BlockSpec((None, 1, w), lambda bi, i: (bi, 0, 0)),
                  full((8, w)), full((1, w)),
                  full((LRU_BLOCKS, LRU_BLOCK, LRU_BLOCK)), full((1, w)),
                  full((LRU_BLOCKS, LRU_BLOCK, LRU_BLOCK)), full((1, w)), full((1, w))],
        out_specs=[pl.BlockSpec((None, tt, w), lambda bi, i: (bi, i, 0)),
                   pl.BlockSpec((None, 8, w), lambda bi, i: (bi, 0, 0)),
                   pl.BlockSpec((None, 8, w), lambda bi, i: (bi, 0, 0))],
        out_shape=[jax.ShapeDtypeStruct((b, t, w), BF16),
                   jax.ShapeDtypeStruct((b, 8, w), F32),
                   jax.ShapeDtypeStruct((b, 8, w), F32)],
        scratch_shapes=[pltpu.VMEM((8, w), F32), pltpu.VMEM((1, w), F32)],
        compiler_params=_cparams(("parallel", "arbitrary")),
        name="lru",
    )(z3, z3, cbuf8, h0.astype(F32).reshape(b, 1, w), cw8, row(cb), wa.astype(BF16), row(ba),
      wx.astype(BF16), row(bx), row(lam))


def _rope_tables(pos, d, period, width=LANES):
    half = d // 2
    inv = ROPE_THETA ** (-jnp.arange(0, d, 2, dtype=F32) / d)
    ang = pos.astype(F32)[:, None] * inv[None, :]
    cos, sin = jnp.cos(ang), jnp.sin(ang)
    t = pos.shape[0]
    cos_p = jnp.concatenate([cos, cos, jnp.ones((t, period - d), F32)], axis=1)
    sin_p = jnp.concatenate([-sin, sin, jnp.zeros((t, period - d), F32)], axis=1)
    reps = width // period
    assert half * 2 == d and reps * period == width
    return jnp.tile(cos_p, (1, reps)), jnp.tile(sin_p, (1, reps))


def _rotate(x, cos_t, sin_t, half, period):
    if 2 * half == LANES:
        partner = pltpu.roll(x, half, axis=1)
    else:
        lane = lax.broadcasted_iota(I32, x.shape, 1)
        partner = jnp.where((lane & (period - 1)) < half,
                            pltpu.roll(x, LANES - half, axis=1), pltpu.roll(x, half, axis=1))
    return x * cos_t + partner * sin_t


def _mla_q_kernel(zq_ref, zkv_ref, zkr_ref, cos_ref, sin_ref, qg_ref, kvg_ref, wq_ref, q_ref, lat_ref, kr_ref):
    cos_t, sin_t = cos_ref[...], sin_ref[...]
    qn = _rms(zq_ref[...], qg_ref[...]).astype(BF16)
    q = jnp.dot(qn, wq_ref[...], preferred_element_type=F32)
    for h in range(MLA_HEADS):
        c0 = h * MLA_QK_PAD
        q_ref[:, c0:c0 + LANES] = q[:, c0:c0 + LANES].astype(BF16)
        q_ref[:, c0 + LANES:c0 + 2 * LANES] = _rotate(
            q[:, c0 + LANES:c0 + 2 * LANES], cos_t, sin_t, MLA_ROPE // 2, MLA_ROPE).astype(BF16)
    lat_ref[...] = _rms(zkv_ref[...], kvg_ref[...])
    kr_ref[...] = _rotate(zkr_ref[...], cos_t, sin_t, MLA_ROPE // 2, MLA_ROPE)


def _mla_q(z, cos_t, sin_t, qg, kvg, wq):
    m = z.shape[0]
    t = cos_t.shape[0]
    tm = _row_tile(t, 256)
    nt = t // tm
    zblk = lambda width, off: pl.BlockSpec((tm, width), lambda i: (i, off // width))
    tab = pl.BlockSpec((tm, LANES), lambda i: (i % nt, 0))
    full = lambda shape: pl.BlockSpec(shape, lambda i: (0,) * len(shape))
    return pl.pallas_call(
        _mla_q_kernel,
        grid=(m // tm,),
        in_specs=[zblk(MLA_Q_LORA, Z_B_Q), zblk(MLA_KV_LORA, Z_B_KV), zblk(LANES, Z_B_KR), tab, tab,
                  full((1, MLA_Q_LORA)), full((1, MLA_KV_LORA)), full((MLA_Q_LORA, MLA_HEADS * MLA_QK_PAD))],
        out_specs=[pl.BlockSpec((tm, MLA_HEADS * MLA_QK_PAD), lambda i: (i, 0)),
                   pl.BlockSpec((tm, MLA_KV_LORA), lambda i: (i, 0)),
                   pl.BlockSpec((tm, LANES), lambda i: (i, 0))],
        out_shape=[jax.ShapeDtypeStruct((m, MLA_HEADS * MLA_QK_PAD), BF16),
                   jax.ShapeDtypeStruct((m, MLA_KV_LORA), F32),
                   jax.ShapeDtypeStruct((m, LANES), F32)],
        compiler_params=_cparams(("parallel",)),
        name="mla_q",
    )(z, z, z, cos_t, sin_t, qg.reshape(1, -1), kvg.reshape(1, -1), wq)


def _mla_kv_kernel(lat_ref, kr_ref, wk_ref, wv_ref, k_ref, v_ref):
    latb = lat_ref[...].astype(BF16)
    k = jnp.dot(latb, wk_ref[...], preferred_element_type=F32)
    v_ref[...] = jnp.dot(latb, wv_ref[...], preferred_element_type=F32).astype(BF16)
    krb = kr_ref[...].astype(BF16)
    for h in range(MLA_HEADS):
        k_ref[:, h * MLA_QK_PAD:h * MLA_QK_PAD + LANES] = k[:, h * MLA_NOPE:(h + 1) * MLA_NOPE].astype(BF16)
        k_ref[:, h * MLA_QK_PAD + LANES:(h + 1) * MLA_QK_PAD] = krb


def _mla_kv(lat_all, kr_all, wk, wv):
    m = lat_all.shape[0]
    tm = 384 if m % 512 else 512
    assert m % tm == 0
    full = lambda shape: pl.BlockSpec(shape, lambda i: (0,) * len(shape))
    return pl.pallas_call(
        _mla_kv_kernel,
        grid=(m // tm,),
        in_specs=[pl.BlockSpec((tm, MLA_KV_LORA), lambda i: (i, 0)), pl.BlockSpec((tm, LANES), lambda i: (i, 0)),
                  full(wk.shape), full(wv.shape)],
        out_specs=[pl.BlockSpec((tm, MLA_HEADS * MLA_QK_PAD), lambda i: (i, 0)),
                   pl.BlockSpec((tm, MLA_HEADS * MLA_V), lambda i: (i, 0))],
        out_shape=[jax.ShapeDtypeStruct((m, MLA_HEADS * MLA_QK_PAD), BF16),
                   jax.ShapeDtypeStruct((m, MLA_HEADS * MLA_V), BF16)],
        compiler_params=_cparams(("parallel",)),
        name="mla_kv",
    )(lat_all, kr_all, wk, wv)


def _dsa_prep_kernel(zq_ref, zk_ref, ziq_ref, zik_ref, ziw_ref, cos_ref, sin_ref, cosp_ref, sinp_ref,
                     ikg_ref, ikb_ref, q_ref, k_ref, iq_ref, ki_ref, iw_ref):
    cos_t, sin_t = cos_ref[...], sin_ref[...]
    cos_p, sin_p = cosp_ref[...], sinp_ref[...]
    for h in range(DSA_HEADS):
        sl = slice(h * DSA_HD, (h + 1) * DSA_HD)
        q_ref[:, sl] = _rotate(zq_ref[:, sl], cos_t, sin_t, DSA_HD // 2, DSA_HD).astype(BF16)
    for h in range(DSA_KV_HEADS):
        sl = slice(h * DSA_HD, (h + 1) * DSA_HD)
        k_ref[:, sl] = _rotate(zk_ref[:, sl], cos_t, sin_t, DSA_HD // 2, DSA_HD)
    lane = lax.broadcasted_iota(I32, cos_t.shape, 1)
    low = lane < IDX_HD
    for c in range(IDX_HEADS // 2):
        y = _rotate(ziq_ref[:, c * LANES:(c + 1) * LANES], cos_p, sin_p, IDX_ROPE // 2, IDX_HD)
        iq_ref[2 * c] = jnp.where(low, y, 0.0).astype(BF16)
        iq_ref[2 * c + 1] = jnp.where(low, pltpu.roll(y, IDX_HD, axis=1), 0.0).astype(BF16)
    x = zik_ref[...]
    mean = jnp.sum(x, axis=-1, keepdims=True) * (1.0 / IDX_HD)
    xc = jnp.where(low, x - mean, 0.0)
    var = jnp.sum(xc * xc, axis=-1, keepdims=True) * (1.0 / IDX_HD)
    y = xc * lax.rsqrt(var + NORM_EPS) * ikg_ref[...] + ikb_ref[...]
    ki_ref[...] = _rotate(y, cos_p, sin_p, IDX_ROPE // 2, IDX_HD)
    iw_ref[...] = ziw_ref[...] * ((IDX_HEADS * IDX_HD) ** -0.5)


def _dsa_prep(z, tabs, ikg, ikb):
    m = z.shape[0]
    cos_t, sin_t, cos_p, sin_p = tabs
    t = cos_t.shape[0]
    tm = _row_tile(t, 256)
    nt = t // tm
    zblk = lambda width, off: pl.BlockSpec((tm, width), lambda i: (i, off // width))
    tab = pl.BlockSpec((tm, LANES), lambda i: (i % nt, 0))
    full = lambda shape: pl.BlockSpec(shape, lambda i: (0,) * len(shape))
    pad = lambda v: jnp.concatenate([v, jnp.zeros((LANES - IDX_HD,), F32)]).reshape(1, LANES)
    return pl.pallas_call(
        _dsa_prep_kernel,
        grid=(m // tm,),
        in_specs=[zblk(1024, Z_C_Q), zblk(512, Z_C_K), zblk(1024, Z_C_IQ), zblk(LANES, Z_C_IK), zblk(LANES, Z_C_IW),
                  tab, tab, tab, tab, full((1, LANES)), full((1, LANES))],
        out_specs=[pl.BlockSpec((tm, 1024), lambda i: (i, 0)),
                   pl.BlockSpec((tm, 512), lambda i: (i, 0)),
                   pl.BlockSpec((IDX_HEADS, tm, LANES), lambda i: (0, i, 0)),
                   pl.BlockSpec((tm, LANES), lambda i: (i, 0)),
                   pl.BlockSpec((tm, LANES), lambda i: (i, 0))],
        out_shape=[jax.ShapeDtypeStruct((m, 1024), BF16),
                   jax.ShapeDtypeStruct((m, 512), F32),
                   jax.ShapeDtypeStruct((IDX_HEADS, m, LANES), BF16),
                   jax.ShapeDtypeStruct((m, LANES), F32),
                   jax.ShapeDtypeStruct((m, LANES), F32)],
        compiler_params=_cparams(("parallel",)),
        name="dsa_prep",
    )(z, z, z, z, z, cos_t, sin_t, cos_p, sin_p, pad(ikg), pad(ikb))


def _visible_tiles(i, *, tq, tk, past, s_valid):
    last = past + (i + 1) * tq - 1
    vis_end = jnp.minimum(s_valid, ((last >> CHUNK_SHIFT) + 1) << CHUNK_SHIFT)
    return (vis_end + tk - 1) // tk


def _select_kernel(iq_ref, iw_ref, ki_ref, mask_ref, skey_ref, thr_ref, jcut_ref, *, tq, tk, n_tiles, past,
                   s_valid, topk):
    i = pl.program_id(1)
    nk = _visible_tiles(i, tq=tq, tk=tk, past=past, s_valid=s_valid)
    qchunk = (past + i * tq + lax.broadcasted_iota(I32, (tq, tk), 0)) >> CHUNK_SHIFT
    lane_k = lax.broadcasted_iota(I32, (tq, tk), 1)
    iw = iw_ref[...]
    wcols = [jnp.broadcast_to(iw[:, h:h + 1], (tq, LANES)) for h in range(IDX_HEADS)]

    def score_body(j, carry):
        kt = ki_ref[pl.ds(j * tk, tk), :]
        acc = [jnp.zeros((tq, LANES), F32) for _ in range(tk // LANES)]
        for h in range(IDX_HEADS):
            lg = lax.dot_general(iq_ref[h], kt, (((1,), (1,)), ((), ())), preferred_element_type=F32)
            for c in range(tk // LANES):
                acc[c] = acc[c] + wcols[h] * jnp.maximum(lg[:, c * LANES:(c + 1) * LANES], 0.0)
        score = jnp.concatenate(acc, axis=1)
        bits = lax.bitcast_convert_type(score, I32)
        key = bits ^ ((bits >> 31) & 0x7FFFFFFF)
        kpos = j * tk + lane_k
        vis = ((kpos >> CHUNK_SHIFT) <= qchunk) & (kpos < s_valid)
        skey_ref[:, pl.ds(j * tk, tk)] = jnp.where(vis, key, INT_MIN)
        return carry

    lax.fori_loop(0, nk, score_body, 0)

    rg = min(tq, 128)
    lane_g = lax.broadcasted_iota(I32, (rg, LANES), 1)

    def count(r0, pred):
        def body(j, cnt):
            t = skey_ref[r0:r0 + rg, pl.ds(j * tk, tk)]
            for c in range(tk // LANES):
                cnt = cnt + jnp.where(pred(t[:, c * LANES:(c + 1) * LANES], j * tk + c * LANES), 1, 0)
            return cnt

        cnt = lax.fori_loop(0, nk, body, jnp.zeros((rg, LANES), I32))
        return jnp.sum(cnt, axis=1, keepdims=True)

    def count_ge(r0, cand):
        cand_b = jnp.broadcast_to(cand, (rg, LANES))
        return count(r0, lambda t, k0: t >= cand_b)

    n_ties = jnp.zeros((1, 1), I32)
    for r0 in range(0, tq, rg):
        c0 = count_ge(r0, jnp.zeros((rg, 1), I32))
        nonneg = c0 >= topk
        prefix = jnp.where(nonneg, 0, INT_MIN).astype(I32)
        n_ge = jnp.where(nonneg, c0, nk * tk)

        def bit_body(it, carry, r0=r0):
            prefix, n_ge = carry
            cand = prefix + lax.shift_left(jnp.int32(1), 30 - it)
            c = count_ge(r0, cand)
            ok = c >= topk
            return jnp.where(ok, cand, prefix), jnp.where(ok, c, n_ge)

        prefix, n_ge = lax.fori_loop(0, 31, bit_body, (prefix, n_ge))
        thr_ref[r0:r0 + rg] = jnp.broadcast_to(jnp.maximum(prefix, INT_MIN + 1), (rg, LANES))
        tied = (n_ge > topk) & (prefix > INT_MIN)
        n_ties = n_ties + jnp.sum(jnp.where(tied, 1, 0), axis=0, keepdims=True)
    jcut_ref[...] = jnp.full(jcut_ref.shape, n_tiles * tk, I32)

    @pl.when(n_ties[0, 0] > 0)
    def _():
        for r0 in range(0, tq, rg):
            thr = thr_ref[r0:r0 + rg]
            need = topk - count(r0, lambda t, k0: t > thr)

            def idx_body(it, jcut, r0=r0, thr=thr, need=need):
                cand = jcut + lax.shift_left(jnp.int32(1), INDEX_BITS - 1 - it)
                cand_b = jnp.broadcast_to(cand, (rg, LANES))
                c = count(r0, lambda t, k0: (t == thr) & (k0 + lane_g < cand_b))
                return jnp.where(c < need, cand, jcut)

            jcut = lax.fori_loop(0, INDEX_BITS, idx_body, jnp.zeros((rg, 1), I32))
            jcut_ref[r0:r0 + rg] = jnp.broadcast_to(jcut, (rg, LANES))

    thr_all, jcut_all = thr_ref[...], jcut_ref[...]
    lane_q = lax.broadcasted_iota(I32, (tq, LANES), 1)

    def mask_body(j, carry):
        t = skey_ref[:, pl.ds(j * tk, tk)]
        cols = []
        for c in range(tk // LANES):
            tc = t[:, c * LANES:(c + 1) * LANES]
            kidx = j * tk + c * LANES + lane_q
            cols.append(jnp.where((tc > thr_all) | ((tc == thr_all) & (kidx <= jcut_all)), 1, 0))
        mask_ref[:, pl.ds(j * tk, tk)] = jnp.concatenate(cols, axis=1).astype(mask_ref.dtype)
        return carry

    lax.fori_loop(0, nk, mask_body, 0)

    def zero_body(j, carry):
        mask_ref[:, pl.ds(j * tk, tk)] = jnp.zeros((tq, tk), mask_ref.dtype)
        return carry

    lax.fori_loop(nk, n_tiles, zero_body, 0)


def _bit_transpose32(rows):
    rows = list(rows)
    j, m = 16, 0x0000FFFF
    while j:
        k = 0
        while k < 32:
            t = (rows[k] ^ lax.shift_right_logical(rows[k + j], j)) & _as_i32(m)
            rows[k] = rows[k] ^ t
            rows[k + j] = rows[k + j] ^ lax.shift_left(t, j)
            k = (k + j + 1) & ~j
        j >>= 1
        m = (m ^ (m << j)) & 0xFFFFFFFF
    return rows


def _as_i32(m):
    return m - (1 << 32) if m >= (1 << 31) else m


def _select_cols_kernel(iq_ref, iwt_ref, ki_ref, mask_ref, skey_ref, jcut_ref, planes_ref, alive_ref, ones_ref, *,
                        tq, tk, n_tiles, past, s_valid, topk):
    i = pl.program_id(1)
    nk = _visible_tiles(i, tq=tq, tk=tk, past=past, s_valid=s_valid)
    sub = 32
    gpt = tk // WORD_KEYS
    qchunk = (past + i * tq + lax.broadcasted_iota(I32, (WORD_KEYS, tq), 1)) >> CHUNK_SHIFT
    row_g = lax.broadcasted_iota(I32, (WORD_KEYS, tq), 0)
    row_k = lax.broadcasted_iota(I32, (tk, tq), 0)
    row_s = lax.broadcasted_iota(I32, (sub, tq), 0)

    def score_body(j, carry, masked):
        for g in range(gpt):
            k0 = j * tk + g * WORD_KEYS
            kt = ki_ref[pl.ds(k0, WORD_KEYS), :]
            acc = jnp.zeros((WORD_KEYS, tq), F32)
            for h in range(IDX_HEADS):
                lg = lax.dot_general(kt, iq_ref[h], (((1,), (1,)), ((), ())), preferred_element_type=F32)
                acc = acc + iwt_ref[h:h + 1, :] * jnp.maximum(lg, 0.0)
            bits = lax.bitcast_convert_type(acc, I32)
            ukey = bits ^ ((bits >> 31) | INT_MIN)
            if masked:
                kpos = k0 + row_g
                vis = ((kpos >> CHUNK_SHIFT) <= qchunk) & (kpos < s_valid)
                ukey = jnp.where(vis, ukey, 0)
            skey_ref[pl.ds(k0, WORD_KEYS), :] = ukey ^ INT_MIN
            slabs = _bit_transpose32([ukey[8 * v:8 * v + 8] for v in range(32)])
            for p in range(32):
                planes_ref[j * gpt + g, p] = slabs[p]
            alive_ref[j * gpt + g] = jnp.full((8, tq), -1, I32)
        return carry

    first_chunk_end = (((past + i * tq) >> CHUNK_SHIFT) + 1) << CHUNK_SHIFT
    n_full = jnp.minimum(jnp.minimum(first_chunk_end // tk, s_valid // tk), nk)
    lax.fori_loop(0, n_full, functools.partial(score_body, masked=False), 0)
    lax.fori_loop(n_full, nk, functools.partial(score_body, masked=True), 0)

    n_pairs = (nk + 1) // 2

    @pl.when(nk < 2 * n_pairs)
    def _():
        for g in range(gpt):
            planes_ref[nk * gpt + g] = jnp.zeros((32, 8, tq), I32)
            alive_ref[nk * gpt + g] = jnp.zeros((8, tq), I32)

    srl = lax.shift_right_logical

    def nibble_counts(x):
        x = x - (srl(x, 1) & 0x55555555)
        return (x & 0x33333333) + (srl(x, 2) & 0x33333333)

    def byte_counts(x):
        return (x & 0x0F0F0F0F) + (srl(x, 4) & 0x0F0F0F0F)

    def byte_sum(x):
        x = (x & 0x00FF00FF) + (srl(x, 8) & 0x00FF00FF)
        return (x + srl(x, 16)) & 0xFFFF

    def popcount(x):
        return byte_sum(byte_counts(nibble_counts(x)))

    def bit_body(p, carry):
        thr_u, need, take_prev = carry
        take_b = jnp.broadcast_to(take_prev, (8, tq)) != 0

        def body(j, cnts):
            nib = []
            for g in range(2 * gpt):
                gi = j * 2 * gpt + g
                prev_ones = ones_ref[gi]
                alive = jnp.where(take_b, prev_ones, alive_ref[gi] ^ prev_ones)
                ones = alive & planes_ref[gi, p]
                alive_ref[gi] = alive
                ones_ref[gi] = ones
                nib.append(nibble_counts(ones))
            return tuple(cnt + byte_counts(nib[2 * a] + nib[2 * a + 1]) for a, cnt in enumerate(cnts))

        cnts = lax.fori_loop(0, n_pairs, body, tuple(jnp.zeros((8, tq), I32) for _ in range(gpt)))
        c = jnp.sum(sum(byte_sum(cnt) for cnt in cnts), axis=0, keepdims=True)
        take = c >= need
        thr_u = thr_u | jnp.where(take, lax.shift_left(jnp.int32(1), 31 - p), 0)
        return thr_u, jnp.where(take, need, need - c), jnp.where(take, 1, 0)

    def clear_body(j, carry):
        for g in range(2 * gpt):
            ones_ref[j * 2 * gpt + g] = jnp.zeros((8, tq), I32)
        return carry

    lax.fori_loop(0, n_pairs, clear_body, 0)
    zero = jnp.zeros((1, tq), I32)
    thr_u, need, take_last = lax.fori_loop(0, 32, bit_body, (zero, zero + topk, zero))
    take_b = jnp.broadcast_to(take_last, (8, tq)) != 0

    def equal_body(j, cnt):
        for g in range(2 * gpt):
            gi = j * 2 * gpt + g
            cnt = cnt + popcount(jnp.where(take_b, ones_ref[gi], alive_ref[gi] ^ ones_ref[gi]))
        return cnt

    n_eq = jnp.sum(lax.fori_loop(0, n_pairs, equal_body, jnp.zeros((8, tq), I32)), axis=0, keepdims=True)
    thr = jnp.maximum(thr_u ^ INT_MIN, INT_MIN + 1)
    tied = (n_eq > need) & (thr_u != 0)
    n_ties = jnp.sum(jnp.where(tied, 1, 0), axis=1, keepdims=True)
    jcut_ref[...] = jnp.full(jcut_ref.shape, n_tiles * tk, I32)

    @pl.when(n_ties[0, 0] > 0)
    def _():
        thr_b = jnp.broadcast_to(thr, (sub, tq))

        def count(pred):
            def body(j, cnt):
                for r in range(tk // sub):
                    k0 = j * tk + r * sub
                    cnt = cnt + jnp.where(pred(skey_ref[pl.ds(k0, sub), :], k0), 1, 0)
                return cnt

            cnt = lax.fori_loop(0, nk, body, jnp.zeros((sub, tq), I32))
            return jnp.sum(cnt, axis=0, keepdims=True)

        def idx_body(it, jcut):
            cand_b = jnp.broadcast_to(jcut + lax.shift_left(jnp.int32(1), INDEX_BITS - 1 - it), (sub, tq))
            c = count(lambda t, k0: (t == thr_b) & (k0 + row_s < cand_b))
            return jnp.where(c < need, cand_b[:1], jcut)

        jcut = lax.fori_loop(0, INDEX_BITS, idx_body, jnp.zeros((1, tq), I32))
        jcut_ref[...] = jnp.broadcast_to(jcut, jcut_ref.shape)

    thr_t = jnp.broadcast_to(thr, (tk, tq))
    jcut_t = jnp.broadcast_to(jcut_ref[:1], (tk, tq))

    def mask_body(j, carry):
        t = skey_ref[pl.ds(j * tk, tk), :]
        keep = (t > thr_t) | ((t == thr_t) & (j * tk + row_k <= jcut_t))
        mask_ref[:, pl.ds(j * tk, tk)] = jnp.where(keep, 1, 0).T.astype(mask_ref.dtype)
        return carry

    lax.fori_loop(0, nk, mask_body, 0)

    def zero_body(j, carry):
        mask_ref[:, pl.ds(j * tk, tk)] = jnp.zeros((tq, tk), mask_ref.dtype)
        return carry

    lax.fori_loop(nk, n_tiles, zero_body, 0)


def _dsa_select(iq, iw, ki_all, *, b, t, past, s_valid, tk):
    s_pad = ki_all.shape[1]
    tq = _row_tile(t, 256)
    nq = t // tq
    topk = min(TOPK_MAX, s_valid // 4)
    assert s_pad < 2 ** INDEX_BITS
    if tq % LANES == 0:
        assert tk % WORD_KEYS == 0 and (s_pad // tk) % 2 == 0
        assert (s_pad // tk) // 2 * 16 <= 255
        kern = functools.partial(_select_cols_kernel, tq=tq, tk=tk, n_tiles=s_pad // tk, past=past,
                                 s_valid=s_valid, topk=topk)
        iwt = iw[:, :IDX_HEADS].reshape(b, t, IDX_HEADS).transpose(0, 2, 1)
        return pl.pallas_call(
            kern,
            grid=(b, nq),
            in_specs=[pl.BlockSpec((IDX_HEADS, tq, LANES), lambda bi, i: (0, bi * nq + i, 0)),
                      pl.BlockSpec((None, IDX_HEADS, tq), lambda bi, i: (bi, 0, i)),
                      pl.BlockSpec((None, s_pad, LANES), lambda bi, i: (bi, 0, 0))],
            out_specs=pl.BlockSpec((None, tq, s_pad), lambda bi, i: (bi, i, 0)),
            out_shape=jax.ShapeDtypeStruct((b, t, s_pad), MASK_DTYPE),
            scratch_shapes=[pltpu.VMEM((s_pad, tq), I32), pltpu.VMEM((8, tq), I32),
                            pltpu.VMEM((s_pad // WORD_KEYS, 32, 8, tq), I32),
                            pltpu.VMEM((s_pad // WORD_KEYS, 8, tq), I32),
                            pltpu.VMEM((s_pad // WORD_KEYS, 8, tq), I32)],
            compiler_params=_cparams(("parallel", "parallel")),
            name="dsa_select",
        )(iq, iwt, ki_all)
    kern = functools.partial(_select_kernel, tq=tq, tk=tk, n_tiles=s_pad // tk, past=past, s_valid=s_valid,
                             topk=topk)
    return pl.pallas_call(
        kern,
        grid=(b, nq),
        in_specs=[pl.BlockSpec((IDX_HEADS, tq, LANES), lambda bi, i: (0, bi * nq + i, 0)),
                  pl.BlockSpec((tq, LANES), lambda bi, i: (bi * nq + i, 0)),
                  pl.BlockSpec((None, s_pad, LANES), lambda bi, i: (bi, 0, 0))],
        out_specs=pl.BlockSpec((None, tq, s_pad), lambda bi, i: (bi, i, 0)),
        out_shape=jax.ShapeDtypeStruct((b, t, s_pad), MASK_DTYPE),
        scratch_shapes=[pltpu.VMEM((tq, s_pad), I32), pltpu.VMEM((tq, LANES), I32), pltpu.VMEM((tq, LANES), I32)],
        compiler_params=_cparams(("parallel", "parallel")),
        name="dsa_select",
    )(iq, iw, ki_all)


def _attn_kernel(*refs, tq, tk, hpg, dqk, dv, past, s_valid, scale, has_mask, kv_shared):
    if has_mask:
        q_ref, k_ref, v_ref, mask_ref, o_ref, m_ref, acc_ref, sa_ref, sb_ref = refs
    else:
        q_ref, k_ref, v_ref, o_ref, m_ref, acc_ref, sa_ref, sb_ref, vis_ref = refs
    i = pl.program_id(2)
    nk = _visible_tiles(i, tq=tq, tk=tk, past=past, s_valid=s_valid)
    m_ref[...] = jnp.full(m_ref.shape, NEG_INF, F32)
    acc_ref[...] = jnp.zeros(acc_ref.shape, F32)
    lane_k = lax.broadcasted_iota(I32, (tq, tk), 1)
    if not has_mask:
        qchunk = (past + i * tq + lax.broadcasted_iota(I32, (tq, tk), 0)) >> CHUNK_SHIFT
        vis_ref[...] = qchunk - (lane_k >> CHUNK_SHIFT)
    c = scale * LOG2_E
    ones_col = jnp.where(lax.broadcasted_iota(I32, (tk, LANES), 1) == 0, 1.0, 0.0).astype(BF16)

    def scores(j, s_ref):
        rows = pl.ds(j * tk, tk)
        for h in range(hpg):
            hk = 0 if kv_shared else h
            kt = k_ref[rows, hk * dqk:(hk + 1) * dqk]
            q = q_ref[:, h * dqk:(h + 1) * dqk]
            s = lax.dot_general(q, kt, (((1,), (1,)), ((), ())), preferred_element_type=F32)
            s_ref[h] = s * c

    def softmax_pv(j, s_ref, masked=True):
        rows = pl.ds(j * tk, tk)
        if not masked:
            keep = None
        elif has_mask:
            keep = mask_ref[:, rows].astype(I32) != 0
        else:
            keep = vis_ref[...] >= ((j * tk) >> CHUNK_SHIFT)
            if s_valid % tk:
                keep = keep & (j * tk + lane_k < s_valid)
        for h in range(hpg):
            hk = 0 if kv_shared else h
            v1 = jnp.concatenate([v_ref[rows, hk * dv:(hk + 1) * dv], ones_col], axis=1)
            s = jnp.where(keep, s_ref[h], NEG_INF) if masked else s_ref[h]
            m_prev = m_ref[h]
            m_new = jnp.maximum(m_prev, jnp.max(s, axis=1, keepdims=True))
            alpha = jnp.exp2(m_prev - m_new)
            p = jnp.exp2(s - jnp.tile(m_new, (1, tk // LANES)))
            pv = jnp.dot(p.astype(BF16), v1, preferred_element_type=F32)
            acc_ref[h] = jnp.tile(alpha, (1, 2)) * acc_ref[h] + pv
            m_ref[h] = m_new

    def pair_body(jj, carry, masked):
        j0 = 2 * jj
        scores(j0 + 1, sb_ref)
        softmax_pv(j0, sa_ref, masked)
        scores(j0 + 2, sa_ref)
        softmax_pv(j0 + 1, sb_ref, masked)
        return carry

    scores(0, sa_ref)
    n_pairs = (nk - 1) // 2
    if has_mask:
        n_free = 0
    else:
        first_chunk_end = (((past + i * tq) >> CHUNK_SHIFT) + 1) << CHUNK_SHIFT
        n_free = jnp.minimum(jnp.minimum(first_chunk_end, s_valid) // (2 * tk), n_pairs)
        lax.fori_loop(0, n_free, functools.partial(pair_body, masked=False), 0)
    lax.fori_loop(n_free, n_pairs, functools.partial(pair_body, masked=True), 0)
    j0 = 2 * n_pairs

    @pl.when(nk - j0 == 2)
    def _():
        scores(j0 + 1, sb_ref)
        softmax_pv(j0, sa_ref)
        softmax_pv(j0 + 1, sb_ref)

    @pl.when(nk - j0 == 1)
    def _():
        softmax_pv(j0, sa_ref)

    for h in range(hpg):
        acc = acc_ref[h]
        o_ref[:, h * dv:(h + 1) * dv] = (acc[:, :dv] / acc[:, dv:dv + 1]).astype(o_ref.dtype)


def _attention(q, k, v, mask, *, groups, hpg, dqk, kv_shared, past, s_valid, tk, scale):
    b, t, _ = q.shape
    s_pad = k.shape[1]
    dv = LANES
    kvh = 1 if kv_shared else hpg
    tq = _row_tile(t, 1024)
    kern = functools.partial(_attn_kernel, tq=tq, tk=tk, hpg=hpg, dqk=dqk, dv=dv, past=past, s_valid=s_valid,
                             scale=scale, has_mask=mask is not None, kv_shared=kv_shared)
    in_specs = [pl.BlockSpec((None, tq, hpg * dqk), lambda bi, g, i: (bi, i, g)),
                pl.BlockSpec((None, s_pad, kvh * dqk), lambda bi, g, i: (bi, 0, g)),
                pl.BlockSpec((None, s_pad, kvh * dv), lambda bi, g, i: (bi, 0, g))]
    args = [q, k, v]
    if mask is not None:
        in_specs.append(pl.BlockSpec((None, tq, s_pad), lambda bi, g, i: (bi, i, 0)))
        args.append(mask)
    return pl.pallas_call(
        kern,
        grid=(b, groups, t // tq),
        in_specs=in_specs,
        out_specs=pl.BlockSpec((None, tq, hpg * dv), lambda bi, g, i: (bi, i, g)),
        out_shape=jax.ShapeDtypeStruct((b, t, groups * hpg * dv), BF16),
        scratch_shapes=[pltpu.VMEM((hpg, tq, LANES), F32), pltpu.VMEM((hpg, tq, 2 * dv), F32),
                        pltpu.VMEM((hpg, tq, tk), F32), pltpu.VMEM((hpg, tq, tk), F32)]
        + ([] if mask is not None else [pltpu.VMEM((tq, tk), I32)]),
        compiler_params=_cparams(("parallel", "parallel", "arbitrary")),
        name="dsa_attention" if mask is not None else "mla_attention",
    )(*args)


def _merge_kernel(x_ref, oa_ref, ob_ref, oc_ref, g0_ref, g1_ref, g2_ref, bg_ref, wa_ref, wb_ref, wc_ref, wo_ref,
                  o_ref):
    merged = None
    for n, (o_r, g_r, w_r) in enumerate(((oa_ref, g0_ref, wa_ref), (ob_ref, g1_ref, wb_ref),
                                         (oc_ref, g2_ref, wc_ref))):
        gate = _sigmoid(g_r[...] + bg_ref[:, n * D_MODEL:(n + 1) * D_MODEL])
        term = gate * jnp.dot(o_r[...], w_r[...], preferred_element_type=F32)
        merged = term if merged is None else merged + term
    o_ref[...] = x_ref[...] + jnp.dot(merged.astype(BF16), wo_ref[...], preferred_element_type=F32)


def _merge(x, oa, ob, oc, z, bg, wa, wb, wc, wo):
    m, d = x.shape
    tm = _row_tile(m, 512)
    rows = pl.BlockSpec((tm, d), lambda i: (i, 0))
    gblk = lambda n: pl.BlockSpec((tm, d), lambda i: (i, Z_G // d + n))
    full = lambda shape: pl.BlockSpec(shape, lambda i: (0,) * len(shape))
    return pl.pallas_call(
        _merge_kernel,
        grid=(m // tm,),
        in_specs=[rows, rows, rows, rows, gblk(0), gblk(1), gblk(2), full((1, 3 * d)),
                  full((d, d)), full((d, d)), full((d, d)), full((d, d))],
        out_specs=rows,
        out_shape=jax.ShapeDtypeStruct((m, d), F32),
        compiler_params=_cparams(("parallel",)),
        name="merge",
    )(x, oa, ob, oc, z, z, z, bg.reshape(1, -1), wa, wb, wc, wo)


def _ffn_kernel(*refs, final):
    if final:
        x_ref, g_ref, wg_ref, wu_ref, wd_ref, gf_ref, o_ref, hn_ref, acc_ref = refs
    else:
        x_ref, g_ref, wg_ref, wu_ref, wd_ref, o_ref, hn_ref, acc_ref = refs
    j = pl.program_id(1)

    @pl.when(j == 0)
    def _():
        hn_ref[...] = _rms(x_ref[...], g_ref[...]).astype(BF16)
        acc_ref[...] = jnp.zeros(acc_ref.shape, F32)

    hn = hn_ref[...]
    gt = jnp.dot(hn, wg_ref[...], preferred_element_type=F32)
    up = jnp.dot(hn, wu_ref[...], preferred_element_type=F32)
    act = (gt * _sigmoid(gt) * up).astype(BF16)
    acc_ref[...] += jnp.dot(act, wd_ref[...], preferred_element_type=F32)

    @pl.when(j == pl.num_programs(1) - 1)
    def _():
        y = x_ref[...] + acc_ref[...]
        o_ref[...] = _rms(y, gf_ref[...]) if final else y


def _ffn(x, g, wg, wu, wd, gf=None):
    m, d = x.shape
    f = wg.shape[1]
    tm, tf = _row_tile(m, 512), f // 2
    final = gf is not None
    in_specs = [pl.BlockSpec((tm, d), lambda i, j: (i, 0)), pl.BlockSpec((1, d), lambda i, j: (0, 0)),
                pl.BlockSpec((d, tf), lambda i, j: (0, j)), pl.BlockSpec((d, tf), lambda i, j: (0, j)),
                pl.BlockSpec((tf, d), lambda i, j: (j, 0))]
    args = [x, g.reshape(1, d), wg, wu, wd]
    if final:
        in_specs.append(pl.BlockSpec((1, d), lambda i, j: (0, 0)))
        args.append(gf.reshape(1, d))
    return pl.pallas_call(
        functools.partial(_ffn_kernel, final=final),
        grid=(m // tm, f // tf),
        in_specs=in_specs,
        out_specs=pl.BlockSpec((tm, d), lambda i, j: (i, 0)),
        out_shape=jax.ShapeDtypeStruct((m, d), F32),
        scratch_shapes=[pltpu.VMEM((tm, d), BF16), pltpu.VMEM((tm, d), F32)],
        compiler_params=_cparams(("parallel", "arbitrary")),
        name="ffn",
    )(*args)


def _pad_cols(w, width):
    return jnp.concatenate([w, jnp.zeros(w.shape[:-1] + (width - w.shape[-1],), w.dtype)], axis=-1)


def _layout_w_in(w):
    a_gate, a_x, b_q, b_kv, b_kr = w[:, 0:1024], w[:, 1024:2048], w[:, 2048:2432], w[:, 2432:2688], w[:, 2688:2752]
    c_q, c_k, c_v, c_iq = w[:, 2752:3776], w[:, 3776:4288], w[:, 4288:4800], w[:, 4800:5824]
    c_ik, c_iw, g = w[:, 5824:5888], w[:, 5888:5904], w[:, 5904:8976]
    out = jnp.concatenate([a_gate, a_x, c_q, c_iq, g, c_k, c_v, b_kv, _pad_cols(b_kr, LANES),
                           _pad_cols(c_ik, LANES), _pad_cols(c_iw, LANES), b_q], axis=1)
    assert out.shape[1] == Z_WIDTH
    return out.astype(BF16)


def _layout_mla(w_q_up, w_kv_up):
    r = w_q_up.shape[0]
    zeros = jnp.zeros((r, MLA_HEADS, MLA_QK_PAD - MLA_NOPE - MLA_ROPE), w_q_up.dtype)
    wq = jnp.concatenate([w_q_up, zeros], axis=-1).reshape(r, MLA_HEADS * MLA_QK_PAD).astype(BF16)
    wk = w_kv_up[:, :, :MLA_NOPE].reshape(MLA_KV_LORA, MLA_HEADS * MLA_NOPE).astype(BF16)
    wv = w_kv_up[:, :, MLA_NOPE:].reshape(MLA_KV_LORA, MLA_HEADS * MLA_V).astype(BF16)
    return wq, wk, wv


def _key_tile(s_valid):
    return 512 if s_valid % 512 == 0 else 384


def _pad_keys(x, s_pad):
    b, s = x.shape[:2]
    if s == s_pad:
        return x
    return jnp.concatenate([x, jnp.zeros((b, s_pad - s) + x.shape[2:], x.dtype)], axis=1)


def _layer(x3, pos, past, w, final_g):
    (attn_norm_g, w_in, b_gates, lru_conv_w, lru_conv_b, lru_wa, lru_ba, lru_wx, lru_bx, lru_lambda,
     mla_q_norm_g, mla_kv_norm_g, mla_w_q_up, mla_w_kv_up, idx_k_norm_g, idx_k_norm_b,
     w_branch_a, w_branch_b, w_branch_c, w_out, ffn_norm_g, w_ffn_gate, w_ffn_up, w_ffn_down) = w
    past_lat, past_kr, past_k, past_v, past_ki, conv_buf, h0 = past
    b, t, d = x3.shape
    m = b * t
    past_len = 0 if past_lat is None else past_lat.shape[1]
    s_valid = past_len + t
    tk = _key_tile(s_valid)
    s_pad = -(-s_valid // tk) * tk
    x = x3.reshape(m, d)

    z = _norm_matmul(x, attn_norm_g, _layout_w_in(w_in))
    z3 = z.reshape(b, t, Z_WIDTH)

    o_a, conv8, h8 = _lru(z3, conv_buf, h0, lru_conv_w, lru_conv_b, lru_wa, lru_ba, lru_wx, lru_bx,
                          lru_lambda.reshape(-1))
    conv_new, h_new = conv8[:, 8 - (CONV_W - 1):], h8[:, 7]

    wq, wk, wv = _layout_mla(mla_w_q_up, mla_w_kv_up)
    cos64, sin64 = _rope_tables(pos, MLA_ROPE, MLA_ROPE)
    q_b, lat_new, kr_pad = _mla_q(z, cos64, sin64, mla_q_norm_g, mla_kv_norm_g, wq)
    lat_new = lat_new.reshape(b, t, MLA_KV_LORA)
    kr_pad = kr_pad.reshape(b, t, LANES)
    kr_new = kr_pad[:, :, :MLA_ROPE]
    if past_lat is None:
        lat_all, kr_all = lat_new, kr_pad
    else:
        lat_all = jnp.concatenate([past_lat.astype(F32), lat_new], axis=1)
        kr_all = jnp.concatenate([_pad_cols(past_kr.astype(F32), LANES), kr_pad], axis=1)
    lat_all, kr_all = _pad_keys(lat_all, s_pad), _pad_keys(kr_all, s_pad)
    k_b, v_b = _mla_kv(lat_all.reshape(b * s_pad, -1), kr_all.reshape(b * s_pad, -1), wk, wv)
    o_b = _attention(q_b.reshape(b, t, -1), k_b.reshape(b, s_pad, -1), v_b.reshape(b, s_pad, -1), None,
                     groups=MLA_HEADS // 2, hpg=2, dqk=MLA_QK_PAD, kv_shared=False, past=past_len,
                     s_valid=s_valid, tk=tk, scale=(MLA_NOPE + MLA_ROPE) ** -0.5)

    tabs = _rope_tables(pos, DSA_HD, DSA_HD) + _rope_tables(pos, IDX_ROPE, IDX_HD)
    q_c, k_new, iq, ki_pad, iw = _dsa_prep(z, tabs, idx_k_norm_g, idx_k_norm_b)
    v_new = z3[:, :, Z_C_V:Z_C_V + DSA_KV_HEADS * DSA_HD]
    k_new = k_new.reshape(b, t, -1)
    ki_pad = ki_pad.reshape(b, t, LANES)
    ki_new = ki_pad[:, :, :IDX_HD]
    if past_k is None:
        k_all, v_all, ki_all = k_new, v_new, ki_pad
    else:
        k_all = jnp.concatenate([past_k.astype(F32).reshape(b, past_len, -1), k_new], axis=1)
        v_all = jnp.concatenate([past_v.astype(F32).reshape(b, past_len, -1), v_new], axis=1)
        ki_all = jnp.concatenate([_pad_cols(past_ki.astype(F32), LANES), ki_pad], axis=1)
    k_all, v_all, ki_all = (_pad_keys(a.astype(BF16), s_pad) for a in (k_all, v_all, ki_all))
    keep = _dsa_select(iq, iw, ki_all, b=b, t=t, past=past_len, s_valid=s_valid, tk=tk)
    o_c = _attention(q_c.reshape(b, t, -1), k_all, v_all, keep, groups=DSA_KV_HEADS,
                     hpg=DSA_HEADS // DSA_KV_HEADS, dqk=DSA_HD, kv_shared=True, past=past_len, s_valid=s_valid,
                     tk=tk, scale=DSA_HD ** -0.5)

    bf = lambda a: a.astype(BF16)
    x = _merge(x, o_a.reshape(m, -1), o_b.reshape(m, -1), o_c.reshape(m, -1), z, b_gates,
               bf(w_branch_a), bf(w_branch_b), bf(w_branch_c), bf(w_out))
    x = _ffn(x, ffn_norm_g, bf(w_ffn_gate), bf(w_ffn_up), bf(w_ffn_down), final_g)
    new = (lat_new, kr_new, k_new.reshape(b, t, DSA_KV_HEADS, DSA_HD), v_new.reshape(b, t, DSA_KV_HEADS, DSA_HD),
           ki_new, conv_new, h_new)
    return x.reshape(b, t, d), new


def _trunk(x, caches, weights, final_norm_g):
    b, t, _ = x.shape
    depth = weights[0].shape[0]
    past_len = 0 if caches is None else caches[0].shape[2]
    pos = past_len + jnp.arange(t, dtype=I32)
    new = []
    for l in range(depth):
        w_l = tuple(wt[l] for wt in weights)
        if caches is None:
            past = (None, None, None, None, None, jnp.zeros((b, CONV_W - 1, LRU_WIDTH), F32),
                    jnp.zeros((b, LRU_WIDTH), F32))
        else:
            past = tuple(c[l] for c in caches)
        x, st = _layer(x, pos, past, w_l, final_norm_g if l == depth - 1 else None)
        new.append(st)
    return x, tuple(jnp.stack([s[i] for s in new]) for i in range(7))


def kernel(x_prompt, x_sample, cache_mla_latent, cache_mla_krope, cache_dsa_k, cache_dsa_v, cache_dsa_kidx,
           state_lru_conv, state_lru_h, attn_norm_g, w_in, b_gates, lru_conv_w, lru_conv_b, lru_wa, lru_ba,
           lru_wx, lru_bx, lru_lambda, mla_q_norm_g, mla_kv_norm_g, mla_w_q_up, mla_w_kv_up, idx_k_norm_g,
           idx_k_norm_b, w_branch_a, w_branch_b, w_branch_c, w_out, ffn_norm_g, w_ffn_gate, w_ffn_up,
           w_ffn_down, final_norm_g):
    weights = (attn_norm_g, w_in, b_gates, lru_conv_w, lru_conv_b, lru_wa, lru_ba, lru_wx, lru_bx, lru_lambda,
               mla_q_norm_g, mla_kv_norm_g, mla_w_q_up, mla_w_kv_up, idx_k_norm_g, idx_k_norm_b,
               w_branch_a, w_branch_b, w_branch_c, w_out, ffn_norm_g, w_ffn_gate, w_ffn_up, w_ffn_down)
    y_p, st_p = _trunk(x_prompt, None, weights, final_norm_g)
    caches = (cache_mla_latent, cache_mla_krope, cache_dsa_k, cache_dsa_v, cache_dsa_kidx, state_lru_conv,
              state_lru_h)
    y_s, st_s = _trunk(x_sample, caches, weights, final_norm_g)
    (lat_p, kr_p, k_p, v_p, ki_p, conv_p, h_p) = st_p
    (lat_s, kr_s, k_s, v_s, ki_s, conv_s, h_s) = st_s
    return (y_p, y_s, lat_p, lat_s, kr_p, kr_s, k_p, k_s, v_p, v_s, ki_p, ki_s, conv_p, conv_s, h_p, h_s)
```

```python
import functools

import jax
import jax.numpy as jnp
from jax import lax
from jax.experimental import pallas as pl
from jax.experimental.pallas import tpu as pltpu

F32 = jnp.float32
BF16 = jnp.bfloat16
I32 = jnp.int32

D_MODEL = 1024
CHUNK_SHIFT = 6
ROPE_THETA = 10000.0
NORM_EPS = 1e-6
NEG_INF = -1e30
INT_MIN = -(2 ** 31)
LOG2_E = 1.4426950408889634
INDEX_BITS = 14
WORD_KEYS = 256
MASK_DTYPE = jnp.int8

LRU_WIDTH = 1024
LRU_BLOCKS = 8
LRU_BLOCK = 128
CONV_W = 4
LRU_C = 8.0

MLA_HEADS = 8
MLA_Q_LORA = 384
MLA_KV_LORA = 256
MLA_NOPE = 128
MLA_ROPE = 64
MLA_V = 128
MLA_QK_PAD = 256

DSA_HEADS = 8
DSA_KV_HEADS = 4
DSA_HD = 128
IDX_HEADS = 16
IDX_HD = 64
IDX_ROPE = 32
TOPK_MAX = 256
D_FF = 2816

LANES = 128
VMEM_LIMIT = 56 * 1024 * 1024

Z_A_GATE, Z_A_X, Z_C_Q, Z_C_IQ, Z_G = 0, 1024, 2048, 3072, 4096
Z_C_K, Z_C_V, Z_B_KV, Z_B_KR, Z_C_IK, Z_C_IW, Z_B_Q = 7168, 7680, 8192, 8448, 8576, 8704, 8832
Z_WIDTH = 9216


def _cparams(sem):
    return pltpu.CompilerParams(dimension_semantics=sem, vmem_limit_bytes=VMEM_LIMIT)


def _sigmoid(x):
    return 1.0 / (1.0 + jnp.exp(-x))


def _gelu_tanh(x):
    return 0.5 * x * (1.0 + jnp.tanh(0.7978845608028654 * (x + 0.044715 * (x * x * x))))


def _rms(x, g):
    return x * lax.rsqrt(jnp.mean(x * x, axis=-1, keepdims=True) + NORM_EPS) * g


def _row_tile(m, pref):
    t = min(m, pref)
    assert m % t == 0, (m, t)
    return t


def _norm_matmul_kernel(x_ref, g_ref, w_ref, o_ref, xn_ref):
    @pl.when(pl.program_id(1) == 0)
    def _():
        xn_ref[...] = _rms(x_ref[...], g_ref[...]).astype(BF16)

    o_ref[...] = jnp.dot(xn_ref[...], w_ref[...], preferred_element_type=F32)


def _norm_matmul(x, g, w):
    m, d = x.shape
    n = w.shape[1]
    tm, tn = _row_tile(m, 2048), 1024
    return pl.pallas_call(
        _norm_matmul_kernel,
        grid=(m // tm, n // tn),
        in_specs=[pl.BlockSpec((tm, d), lambda i, j: (i, 0)),
                  pl.BlockSpec((1, d), lambda i, j: (0, 0)),
                  pl.BlockSpec((d, tn), lambda i, j: (0, j))],
        out_specs=pl.BlockSpec((tm, tn), lambda i, j: (i, j)),
        out_shape=jax.ShapeDtypeStruct((m, n), F32),
        scratch_shapes=[pltpu.VMEM((tm, d), BF16)],
        compiler_params=_cparams(("parallel", "arbitrary")),
        name="norm_matmul",
    )(x, g.reshape(1, d), w)


def _lru_kernel(gate_ref, xin_ref, cbuf_ref, h0_ref, cw_ref, cb_ref, wa_ref, ba_ref, wx_ref, bx_ref, lam_ref,
                o_ref, clast_ref, hlast_ref, prev_ref, hc_ref, *, tt):
    @pl.when(pl.program_id(1) == 0)
    def _():
        prev_ref[...] = cbuf_ref[...]
        hc_ref[...] = h0_ref[...]

    row = lax.broadcasted_iota(I32, (tt, LRU_BLOCK), 0)
    row8 = lax.broadcasted_iota(I32, (8, LRU_BLOCK), 0)
    for n in range(LRU_BLOCKS):
        sl = slice(n * LRU_BLOCK, (n + 1) * LRU_BLOCK)
        x = xin_ref[:, sl]
        prev = prev_ref[:, sl]
        u = cb_ref[:, sl]
        for j in range(CONV_W):
            d = CONV_W - 1 - j
            if d == 0:
                xs = x
            else:
                rx = pltpu.roll(x, d, axis=0)
                head = jnp.where(row8 < d, pltpu.roll(prev, d, axis=0), rx[:8])
                xs = head if tt == 8 else jnp.concatenate([head, rx[8:]], axis=0)
            u = u + xs * cw_ref[j:j + 1, sl]
        ub = u.astype(BF16)
        r = _sigmoid(jnp.dot(ub, wa_ref[n], preferred_element_type=F32) + ba_ref[:, sl])
        ig = _sigmoid(jnp.dot(ub, wx_ref[n], preferred_element_type=F32) + bx_ref[:, sl])
        lam = lam_ref[:, sl]
        log_sig = jnp.minimum(lam, 0.0) - jnp.log1p(jnp.exp(-jnp.abs(lam)))
        a = jnp.exp(LRU_C * r * log_sig)
        b = jnp.sqrt(1.0 - a * a) * (ig * u)
        d = 1
        while d < tt:
            if d < 8:
                keep = row >= d
                b = jnp.where(keep, a * pltpu.roll(b, d, axis=0) + b, b)
                a = jnp.where(keep, a * pltpu.roll(a, d, axis=0), a)
            else:
                b = jnp.concatenate([b[:d], a[d:] * b[:tt - d] + b[d:]], axis=0)
                a = jnp.concatenate([a[:d], a[d:] * a[:tt - d]], axis=0)
            d *= 2
        h = a * hc_ref[:, sl] + b
        hc_ref[:, sl] = h[tt - 1:tt]
        o_ref[:, sl] = (h * _gelu_tanh(gate_ref[:, sl])).astype(o_ref.dtype)
        hlast_ref[:, sl] = h[tt - 8:]
        clast_ref[:, sl] = x[tt - 8:]
        prev_ref[:, sl] = x[tt - 8:]


def _lru(z3, conv_buf, h0, cw, cb, wa, ba, wx, bx, lam):
    b, t, _ = z3.shape
    w = LRU_WIDTH
    tt = _row_tile(t, 256)
    assert tt % 8 == 0 and tt & (tt - 1) == 0
    cbuf8 = jnp.concatenate([jnp.zeros((b, 8 - (CONV_W - 1), w), F32), conv_buf.astype(F32)], axis=1)
    cw8 = jnp.concatenate([cw, jnp.zeros((8 - CONV_W, w), F32)], axis=0)
    row = lambda v: v.reshape(1, w)
    full = lambda shape: pl.BlockSpec(shape, lambda bi, i: (0,) * len(shape))
    return pl.pallas_call(
        functools.partial(_lru_kernel, tt=tt),
        grid=(b, t // tt),
        in_specs=[pl.BlockSpec((None, tt, w), lambda bi, i: (bi, i, Z_A_GATE // w)),
                  pl.BlockSpec((None, tt, w), lambda bi, i: (bi, i, Z_A_X // w)),
                  pl.BlockSpec((None, 8, w), lambda bi, i: (bi, 0, 0)),
                  pl.BlockSpec((None, 1, w), lambda bi, i: (bi, 0, 0)),
                  full((8, w)), full((1, w)),
                  full((LRU_BLOCKS, LRU_BLOCK, LRU_BLOCK)), full((1, w)),
                  full((LRU_BLOCKS, LRU_BLOCK, LRU_BLOCK)), full((1, w)), full((1, w))],
        out_specs=[pl.BlockSpec((None, tt, w), lambda bi, i: (bi, i, 0)),
                   pl.BlockSpec((None, 8, w), lambda bi, i: (bi, 0, 0)),
                   pl.BlockSpec((None, 8, w), lambda bi, i: (bi, 0, 0))],
        out_shape=[jax.ShapeDtypeStruct((b, t, w), BF16),
                   jax.ShapeDtypeStruct((b, 8, w), F32),
                   jax.ShapeDtypeStruct((b, 8, w), F32)],
        scratch_shapes=[pltpu.VMEM((8, w), F32), pltpu.VMEM((1, w), F32)],
        compiler_params=_cparams(("parallel", "arbitrary")),
        name="lru",
    )(z3, z3, cbuf8, h0.astype(F32).reshape(b, 1, w), cw8, row(cb), wa.astype(BF16), row(ba),
      wx.astype(BF16), row(bx), row(lam))


def _rope_tables(pos, d, period, width=LANES):
    half = d // 2
    inv = ROPE_THETA ** (-jnp.arange(0, d, 2, dtype=F32) / d)
    ang = pos.astype(F32)[:, None] * inv[None, :]
    cos, sin = jnp.cos(ang), jnp.sin(ang)
    t = pos.shape[0]
    cos_p = jnp.concatenate([cos, cos, jnp.ones((t, period - d), F32)], axis=1)
    sin_p = jnp.concatenate([-sin, sin, jnp.zeros((t, period - d), F32)], axis=1)
    reps = width // period
    assert half * 2 == d and reps * period == width
    return jnp.tile(cos_p, (1, reps)), jnp.tile(sin_p, (1, reps))


def _rotate(x, cos_t, sin_t, half, period):
    if 2 * half == LANES:
        partner = pltpu.roll(x, half, axis=1)
    else:
        lane = lax.broadcasted_iota(I32, x.shape, 1)
        partner = jnp.where((lane & (period - 1)) < half,
                            pltpu.roll(x, LANES - half, axis=1), pltpu.roll(x, half, axis=1))
    return x * cos_t + partner * sin_t


def _mla_q_kernel(zq_ref, zkv_ref, zkr_ref, cos_ref, sin_ref, qg_ref, kvg_ref, wq_ref, q_ref, lat_ref, kr_ref):
    cos_t, sin_t = cos_ref[...], sin_ref[...]
    qn = _rms(zq_ref[...], qg_ref[...]).astype(BF16)
    q = jnp.dot(qn, wq_ref[...], preferred_element_type=F32)
    for h in range(MLA_HEADS):
        c0 = h * MLA_QK_PAD
        q_ref[:, c0:c0 + LANES] = q[:, c0:c0 + LANES].astype(BF16)
        q_ref[:, c0 + LANES:c0 + 2 * LANES] = _rotate(
            q[:, c0 + LANES:c0 + 2 * LANES], cos_t, sin_t, MLA_ROPE // 2, MLA_ROPE).astype(BF16)
    lat_ref[...] = _rms(zkv_ref[...], kvg_ref[...])
    kr_ref[...] = _rotate(zkr_ref[...], cos_t, sin_t, MLA_ROPE // 2, MLA_ROPE)


def _mla_q(z, cos_t, sin_t, qg, kvg, wq):
    m = z.shape[0]
    t = cos_t.shape[0]
    tm = _row_tile(t, 512)
    nt = t // tm
    zblk = lambda width, off: pl.BlockSpec((tm, width), lambda i: (i, off // width))
    tab = pl.BlockSpec((tm, LANES), lambda i: (i % nt, 0))
    full = lambda shape: pl.BlockSpec(shape, lambda i: (0,) * len(shape))
    return pl.pallas_call(
        _mla_q_kernel,
        grid=(m // tm,),
        in_specs=[zblk(MLA_Q_LORA, Z_B_Q), zblk(MLA_KV_LORA, Z_B_KV), zblk(LANES, Z_B_KR), tab, tab,
                  full((1, MLA_Q_LORA)), full((1, MLA_KV_LORA)), full((MLA_Q_LORA, MLA_HEADS * MLA_QK_PAD))],
        out_specs=[pl.BlockSpec((tm, MLA_HEADS * MLA_QK_PAD), lambda i: (i, 0)),
                   pl.BlockSpec((tm, MLA_KV_LORA), lambda i: (i, 0)),
                   pl.BlockSpec((tm, LANES), lambda i: (i, 0))],
        out_shape=[jax.ShapeDtypeStruct((m, MLA_HEADS * MLA_QK_PAD), BF16),
                   jax.ShapeDtypeStruct((m, MLA_KV_LORA), F32),
                   jax.ShapeDtypeStruct((m, LANES), F32)],
        compiler_params=_cparams(("parallel",)),
        name="mla_q",
    )(z, z, z, cos_t, sin_t, qg.reshape(1, -1), kvg.reshape(1, -1), wq)


def _mla_kv_kernel(lat_ref, kr_ref, wk_ref, wv_ref, k_ref, v_ref):
    latb = lat_ref[...].astype(BF16)
    k = jnp.dot(latb, wk_ref[...], preferred_element_type=F32)
    v_ref[...] = jnp.dot(latb, wv_ref[...], preferred_element_type=F32).astype(BF16)
    krb = kr_ref[...].astype(BF16)
    for h in range(MLA_HEADS):
        k_ref[:, h * MLA_QK_PAD:h * MLA_QK_PAD + LANES] = k[:, h * MLA_NOPE:(h + 1) * MLA_NOPE].astype(BF16)
        k_ref[:, h * MLA_QK_PAD + LANES:(h + 1) * MLA_QK_PAD] = krb


def _mla_kv(lat_all, kr_all, wk, wv):
    m = lat_all.shape[0]
    tm = 384 if m % 512 else 512
    assert m % tm == 0
    full = lambda shape: pl.BlockSpec(shape, lambda i: (0,) * len(shape))
    return pl.pallas_call(
        _mla_kv_kernel,
        grid=(m // tm,),
        in_specs=[pl.BlockSpec((tm, MLA_KV_LORA), lambda i: (i, 0)), pl.BlockSpec((tm, LANES), lambda i: (i, 0)),
                  full(wk.shape), full(wv.shape)],
        out_specs=[pl.BlockSpec((tm, MLA_HEADS * MLA_QK_PAD), lambda i: (i, 0)),
                   pl.BlockSpec((tm, MLA_HEADS * MLA_V), lambda i: (i, 0))],
        out_shape=[jax.ShapeDtypeStruct((m, MLA_HEADS * MLA_QK_PAD), BF16),
                   jax.ShapeDtypeStruct((m, MLA_HEADS * MLA_V), BF16)],
        compiler_params=_cparams(("parallel",)),
        name="mla_kv",
    )(lat_all, kr_all, wk, wv)


def _dsa_prep_kernel(zq_ref, zk_ref, ziq_ref, zik_ref, ziw_ref, cos_ref, sin_ref, cosp_ref, sinp_ref,
                     ikg_ref, ikb_ref, q_ref, k_ref, iq_ref, ki_ref, iw_ref):
    cos_t, sin_t = cos_ref[...], sin_ref[...]
    cos_p, sin_p = cosp_ref[...], sinp_ref[...]
    for h in range(DSA_HEADS):
        sl = slice(h * DSA_HD, (h + 1) * DSA_HD)
        q_ref[:, sl] = _rotate(zq_ref[:, sl], cos_t, sin_t, DSA_HD // 2, DSA_HD).astype(BF16)
    for h in range(DSA_KV_HEADS):
        sl = slice(h * DSA_HD, (h + 1) * DSA_HD)
        k_ref[:, sl] = _rotate(zk_ref[:, sl], cos_t, sin_t, DSA_HD // 2, DSA_HD)
    lane = lax.broadcasted_iota(I32, cos_t.shape, 1)
    low = lane < IDX_HD
    for c in range(IDX_HEADS // 2):
        y = _rotate(ziq_ref[:, c * LANES:(c + 1) * LANES], cos_p, sin_p, IDX_ROPE // 2, IDX_HD)
        iq_ref[2 * c] = jnp.where(low, y, 0.0).astype(BF16)
        iq_ref[2 * c + 1] = jnp.where(low, pltpu.roll(y, IDX_HD, axis=1), 0.0).astype(BF16)
    x = zik_ref[...]
    mean = jnp.sum(x, axis=-1, keepdims=True) * (1.0 / IDX_HD)
    xc = jnp.where(low, x - mean, 0.0)
    var = jnp.sum(xc * xc, axis=-1, keepdims=True) * (1.0 / IDX_HD)
    y = xc * lax.rsqrt(var + NORM_EPS) * ikg_ref[...] + ikb_ref[...]
    ki_ref[...] = _rotate(y, cos_p, sin_p, IDX_ROPE // 2, IDX_HD)
    iw_ref[...] = ziw_ref[...] * ((IDX_HEADS * IDX_HD) ** -0.5)


def _dsa_prep(z, tabs, ikg, ikb):
    m = z.shape[0]
    cos_t, sin_t, cos_p, sin_p = tabs
    t = cos_t.shape[0]
    tm = _row_tile(t, 512)
    nt = t // tm
    zblk = lambda width, off: pl.BlockSpec((tm, width), lambda i: (i, off // width))
    tab = pl.BlockSpec((tm, LANES), lambda i: (i % nt, 0))
    full = lambda shape: pl.BlockSpec(shape, lambda i: (0,) * len(shape))
    pad = lambda v: jnp.concatenate([v, jnp.zeros((LANES - IDX_HD,), F32)]).reshape(1, LANES)
    return pl.pallas_call(
        _dsa_prep_kernel,
        grid=(m // tm,),
        in_specs=[zblk(1024, Z_C_Q), zblk(512, Z_C_K), zblk(1024, Z_C_IQ), zblk(LANES, Z_C_IK), zblk(LANES, Z_C_IW),
                  tab, tab, tab, tab, full((1, LANES)), full((1, LANES))],
        out_specs=[pl.BlockSpec((tm, 1024), lambda i: (i, 0)),
                   pl.BlockSpec((tm, 512), lambda i: (i, 0)),
                   pl.BlockSpec((IDX_HEADS, tm, LANES), lambda i: (0, i, 0)),
                   pl.BlockSpec((tm, LANES), lambda i: (i, 0)),
                   pl.BlockSpec((tm, LANES), lambda i: (i, 0))],
        out_shape=[jax.ShapeDtypeStruct((m, 1024), BF16),
                   jax.ShapeDtypeStruct((m, 512), F32),
                   jax.ShapeDtypeStruct((IDX_HEADS, m, LANES), BF16),
                   jax.ShapeDtypeStruct((m, LANES), F32),
                   jax.ShapeDtypeStruct((m, LANES), F32)],
        compiler_params=_cparams(("parallel",)),
        name="dsa_prep",
    )(z, z, z, z, z, cos_t, sin_t, cos_p, sin_p, pad(ikg), pad(ikb))


def _visible_tiles(i, *, tq, tk, past, s_valid):
    last = past + (i + 1) * tq - 1
    vis_end = jnp.minimum(s_valid, ((last >> CHUNK_SHIFT) + 1) << CHUNK_SHIFT)
    return (vis_end + tk - 1) // tk


def _select_kernel(iq_ref, iw_ref, ki_ref, mask_ref, skey_ref, thr_ref, jcut_ref, *, tq, tk, n_tiles, past,
                   s_valid, topk):
    i = pl.program_id(1)
    nk = _visible_tiles(i, tq=tq, tk=tk, past=past, s_valid=s_valid)
    qchunk = (past + i * tq + lax.broadcasted_iota(I32, (tq, tk), 0)) >> CHUNK_SHIFT
    lane_k = lax.broadcasted_iota(I32, (tq, tk), 1)
    iw = iw_ref[...]
    wcols = [jnp.broadcast_to(iw[:, h:h + 1], (tq, LANES)) for h in range(IDX_HEADS)]

    def score_body(j, carry):
        kt = ki_ref[pl.ds(j * tk, tk), :]
        acc = [jnp.zeros((tq, LANES), F32) for _ in range(tk // LANES)]
        for h in range(IDX_HEADS):
            lg = lax.dot_general(iq_ref[h], kt, (((1,), (1,)), ((), ())), preferred_element_type=F32)
            for c in range(tk // LANES):
                acc[c] = acc[c] + wcols[h] * jnp.maximum(lg[:, c * LANES:(c + 1) * LANES], 0.0)
        score = jnp.concatenate(acc, axis=1)
        bits = lax.bitcast_convert_type(score, I32)
        key = bits ^ ((bits >> 31) & 0x7FFFFFFF)
        kpos = j * tk + lane_k
        vis = ((kpos >> CHUNK_SHIFT) <= qchunk) & (kpos < s_valid)
        skey_ref[:, pl.ds(j * tk, tk)] = jnp.where(vis, key, INT_MIN)
        return carry

    lax.fori_loop(0, nk, score_body, 0)

    rg = min(tq, 128)
    lane_g = lax.broadcasted_iota(I32, (rg, LANES), 1)

    def count(r0, pred):
        def body(j, cnt):
            t = skey_ref[r0:r0 + rg, pl.ds(j * tk, tk)]
            for c in range(tk // LANES):
                cnt = cnt + jnp.where(pred(t[:, c * LANES:(c + 1) * LANES], j * tk + c * LANES), 1, 0)
            return cnt

        cnt = lax.fori_loop(0, nk, body, jnp.zeros((rg, LANES), I32))
        return jnp.sum(cnt, axis=1, keepdims=True)

    def count_ge(r0, cand):
        cand_b = jnp.broadcast_to(cand, (rg, LANES))
        return count(r0, lambda t, k0: t >= cand_b)

    n_ties = jnp.zeros((1, 1), I32)
    for r0 in range(0, tq, rg):
        c0 = count_ge(r0, jnp.zeros((rg, 1), I32))
        nonneg = c0 >= topk
        prefix = jnp.where(nonneg, 0, INT_MIN).astype(I32)
        n_ge = jnp.where(nonneg, c0, nk * tk)

        def bit_body(it, carry, r0=r0):
            prefix, n_ge = carry
            cand = prefix + lax.shift_left(jnp.int32(1), 30 - it)
            c = count_ge(r0, cand)
            ok = c >= topk
            return jnp.where(ok, cand, prefix), jnp.where(ok, c, n_ge)

        prefix, n_ge = lax.fori_loop(0, 31, bit_body, (prefix, n_ge))
        thr_ref[r0:r0 + rg] = jnp.broadcast_to(jnp.maximum(prefix, INT_MIN + 1), (rg, LANES))
        tied = (n_ge > topk) & (prefix > INT_MIN)
        n_ties = n_ties + jnp.sum(jnp.where(tied, 1, 0), axis=0, keepdims=True)
    jcut_ref[...] = jnp.full(jcut_ref.shape, n_tiles * tk, I32)

    @pl.when(n_ties[0, 0] > 0)
    def _():
        for r0 in range(0, tq, rg):
            thr = thr_ref[r0:r0 + rg]
            need = topk - count(r0, lambda t, k0: t > thr)

            def idx_body(it, jcut, r0=r0, thr=thr, need=need):
                cand = jcut + lax.shift_left(jnp.int32(1), INDEX_BITS - 1 - it)
                cand_b = jnp.broadcast_to(cand, (rg, LANES))
                c = count(r0, lambda t, k0: (t == thr) & (k0 + lane_g < cand_b))
                return jnp.where(c < need, cand, jcut)

            jcut = lax.fori_loop(0, INDEX_BITS, idx_body, jnp.zeros((rg, 1), I32))
            jcut_ref[r0:r0 + rg] = jnp.broadcast_to(jcut, (rg, LANES))

    thr_all, jcut_all = thr_ref[...], jcut_ref[...]
    lane_q = lax.broadcasted_iota(I32, (tq, LANES), 1)

    def mask_body(j, carry):
        t = skey_ref[:, pl.ds(j * tk, tk)]
        cols = []
        for c in range(tk // LANES):
            tc = t[:, c * LANES:(c + 1) * LANES]
            kidx = j * tk + c * LANES + lane_q
            cols.append(jnp.where((tc > thr_all) | ((tc == thr_all) & (kidx <= jcut_all)), 1, 0))
        mask_ref[:, pl.ds(j * tk, tk)] = jnp.concatenate(cols, axis=1).astype(mask_ref.dtype)
        return carry

    lax.fori_loop(0, nk, mask_body, 0)

    def zero_body(j, carry):
        mask_ref[:, pl.ds(j * tk, tk)] = jnp.zeros((tq, tk), mask_ref.dtype)
        return carry

    lax.fori_loop(nk, n_tiles, zero_body, 0)


def _bit_transpose32(rows):
    rows = list(rows)
    j, m = 16, 0x0000FFFF
    while j:
        k = 0
        while k < 32:
            t = (rows[k] ^ lax.shift_right_logical(rows[k + j], j)) & _as_i32(m)
            rows[k] = rows[k] ^ t
            rows[k + j] = rows[k + j] ^ lax.shift_left(t, j)
            k = (k + j + 1) & ~j
        j >>= 1
        m = (m ^ (m << j)) & 0xFFFFFFFF
    return rows


def _as_i32(m):
    return m - (1 << 32) if m >= (1 << 31) else m


def _select_cols_kernel(iq_ref, iwt_ref, ki_ref, mask_ref, skey_ref, jcut_ref, planes_ref, alive_ref, ones_ref, *,
                        tq, tk, n_tiles, past, s_valid, topk):
    i = pl.program_id(1)
    nk = _visible_tiles(i, tq=tq, tk=tk, past=past, s_valid=s_valid)
    sub = 32
    gpt = tk // WORD_KEYS
    qchunk = (past + i * tq + lax.broadcasted_iota(I32, (WORD_KEYS, tq), 1)) >> CHUNK_SHIFT
    row_g = lax.broadcasted_iota(I32, (WORD_KEYS, tq), 0)
    row_k = lax.broadcasted_iota(I32, (tk, tq), 0)
    row_s = lax.broadcasted_iota(I32, (sub, tq), 0)

    def score_body(j, carry, masked):
        for g in range(gpt):
            k0 = j * tk + g * WORD_KEYS
            kt = ki_ref[pl.ds(k0, WORD_KEYS), :]
            acc = jnp.zeros((WORD_KEYS, tq), F32)
            for h in range(IDX_HEADS):
                lg = lax.dot_general(kt, iq_ref[h], (((1,), (1,)), ((), ())), preferred_element_type=F32)
                acc = acc + iwt_ref[h:h + 1, :] * jnp.maximum(lg, 0.0)
            bits = lax.bitcast_convert_type(acc, I32)
            ukey = bits ^ ((bits >> 31) | INT_MIN)
            if masked:
                kpos = k0 + row_g
                vis = ((kpos >> CHUNK_SHIFT) <= qchunk) & (kpos < s_valid)
                ukey = jnp.where(vis, ukey, 0)
            skey_ref[pl.ds(k0, WORD_KEYS), :] = ukey ^ INT_MIN
            slabs = _bit_transpose32([ukey[8 * v:8 * v + 8] for v in range(32)])
            for p in range(32):
                planes_ref[j * gpt + g, p] = slabs[p]
            alive_ref[j * gpt + g] = jnp.full((8, tq), -1, I32)
        return carry

    first_chunk_end = (((past + i * tq) >> CHUNK_SHIFT) + 1) << CHUNK_SHIFT
    n_full = jnp.minimum(jnp.minimum(first_chunk_end // tk, s_valid // tk), nk)
    lax.fori_loop(0, n_full, functools.partial(score_body, masked=False), 0)
    lax.fori_loop(n_full, nk, functools.partial(score_body, masked=True), 0)

    n_pairs = (nk + 1) // 2

    @pl.when(nk < 2 * n_pairs)
    def _():
        for g in range(gpt):
            planes_ref[nk * gpt + g] = jnp.zeros((32, 8, tq), I32)
            alive_ref[nk * gpt + g] = jnp.zeros((8, tq), I32)

    srl = lax.shift_right_logical

    def nibble_counts(x):
        x = x - (srl(x, 1) & 0x55555555)
        return (x & 0x33333333) + (srl(x, 2) & 0x33333333)

    def byte_counts(x):
        return (x & 0x0F0F0F0F) + (srl(x, 4) & 0x0F0F0F0F)

    def byte_sum(x):
        x = (x & 0x00FF00FF) + (srl(x, 8) & 0x00FF00FF)
        return (x + srl(x, 16)) & 0xFFFF

    def popcount(x):
        return byte_sum(byte_counts(nibble_counts(x)))

    def bit_body(p, carry):
        thr_u, need, take_prev = carry
        take_b = jnp.broadcast_to(take_prev, (8, tq)) != 0

        def body(j, cnts):
            nib = []
            for g in range(2 * gpt):
                gi = j * 2 * gpt + g
                prev_ones = ones_ref[gi]
                alive = jnp.where(take_b, prev_ones, alive_ref[gi] ^ prev_ones)
                ones = alive & planes_ref[gi, p]
                alive_ref[gi] = alive
                ones_ref[gi] = ones
                nib.append(nibble_counts(ones))
            return tuple(cnt + byte_counts(nib[2 * a] + nib[2 * a + 1]) for a, cnt in enumerate(cnts))

        cnts = lax.fori_loop(0, n_pairs, body, tuple(jnp.zeros((8, tq), I32) for _ in range(gpt)))
        c = jnp.sum(sum(byte_sum(cnt) for cnt in cnts), axis=0, keepdims=True)
        take = c >= need
        thr_u = thr_u | jnp.where(take, lax.shift_left(jnp.int32(1), 31 - p), 0)
        return thr_u, jnp.where(take, need, need - c), jnp.where(take, 1, 0)

    def clear_body(j, carry):
        for g in range(2 * gpt):
            ones_ref[j * 2 * gpt + g] = jnp.zeros((8, tq), I32)
        return carry

    lax.fori_loop(0, n_pairs, clear_body, 0)
    zero = jnp.zeros((1, tq), I32)
    thr_u, need, take_last = lax.fori_loop(0, 32, bit_body, (zero, zero + topk, zero))
    take_b = jnp.broadcast_to(take_last, (8, tq)) != 0

    def equal_body(j, cnt):
        for g in range(2 * gpt):
            gi = j * 2 * gpt + g
            cnt = cnt + popcount(jnp.where(take_b, ones_ref[gi], alive_ref[gi] ^ ones_ref[gi]))
        return cnt

    n_eq = jnp.sum(lax.fori_loop(0, n_pairs, equal_body, jnp.zeros((8, tq), I32)), axis=0, keepdims=True)
    thr = jnp.maximum(thr_u ^ INT_MIN, INT_MIN + 1)
    tied = (n_eq > need) & (thr_u != 0)
    n_ties = jnp.sum(jnp.where(tied, 1, 0), axis=1, keepdims=True)
    jcut_ref[...] = jnp.full(jcut_ref.shape, n_tiles * tk, I32)

    @pl.when(n_ties[0, 0] > 0)
    def _():
        thr_b = jnp.broadcast_to(thr, (sub, tq))

        def count(pred):
            def body(j, cnt):
                for r in range(tk // sub):
                    k0 = j * tk + r * sub
                    cnt = cnt + jnp.where(pred(skey_ref[pl.ds(k0, sub), :], k0), 1, 0)
                return cnt

            cnt = lax.fori_loop(0, nk, body, jnp.zeros((sub, tq), I32))
            return jnp.sum(cnt, axis=0, keepdims=True)

        def idx_body(it, jcut):
            cand_b = jnp.broadcast_to(jcut + lax.shift_left(jnp.int32(1), INDEX_BITS - 1 - it), (sub, tq))
            c = count(lambda t, k0: (t == thr_b) & (k0 + row_s < cand_b))
            return jnp.where(c < need, cand_b[:1], jcut)

        jcut = lax.fori_loop(0, INDEX_BITS, idx_body, jnp.zeros((1, tq), I32))
        jcut_ref[...] = jnp.broadcast_to(jcut, jcut_ref.shape)

    thr_t = jnp.broadcast_to(thr, (tk, tq))
    jcut_t = jnp.broadcast_to(jcut_ref[:1], (tk, tq))

    def mask_body(j, carry):
        t = skey_ref[pl.ds(j * tk, tk), :]
        keep = (t > thr_t) | ((t == thr_t) & (j * tk + row_k <= jcut_t))
        mask_ref[:, pl.ds(j * tk, tk)] = jnp.where(keep, 1, 0).T.astype(mask_ref.dtype)
        return carry

    lax.fori_loop(0, nk, mask_body, 0)

    def zero_body(j, carry):
        mask_ref[:, pl.ds(j * tk, tk)] = jnp.zeros((tq, tk), mask_ref.dtype)
        return carry

    lax.fori_loop(nk, n_tiles, zero_body, 0)


def _dsa_select(iq, iw, ki_all, *, b, t, past, s_valid, tk):
    s_pad = ki_all.shape[1]
    tq = _row_tile(t, 256)
    nq = t // tq
    topk = min(TOPK_MAX, s_valid // 4)
    assert s_pad < 2 ** INDEX_BITS
    if tq % LANES == 0:
        assert tk % WORD_KEYS == 0 and (s_pad // tk) % 2 == 0
        assert (s_pad // tk) // 2 * 16 <= 255
        kern = functools.partial(_select_cols_kernel, tq=tq, tk=tk, n_tiles=s_pad // tk, past=past,
                                 s_valid=s_valid, topk=topk)
        iwt = iw[:, :IDX_HEADS].reshape(b, t, IDX_HEADS).transpose(0, 2, 1)
        return pl.pallas_call(
            kern,
            grid=(b, nq),
            in_specs=[pl.BlockSpec((IDX_HEADS, tq, LANES), lambda bi, i: (0, bi * nq + i, 0)),
                      pl.BlockSpec((None, IDX_HEADS, tq), lambda bi, i: (bi, 0, i)),
                      pl.BlockSpec((None, s_pad, LANES), lambda bi, i: (bi, 0, 0))],
            out_specs=pl.BlockSpec((None, tq, s_pad), lambda bi, i: (bi, i, 0)),
            out_shape=jax.ShapeDtypeStruct((b, t, s_pad), MASK_DTYPE),
            scratch_shapes=[pltpu.VMEM((s_pad, tq), I32), pltpu.VMEM((8, tq), I32),
                            pltpu.VMEM((s_pad // WORD_KEYS, 32, 8, tq), I32),
                            pltpu.VMEM((s_pad // WORD_KEYS, 8, tq), I32),
                            pltpu.VMEM((s_pad // WORD_KEYS, 8, tq), I32)],
            compiler_params=_cparams(("parallel", "parallel")),
            name="dsa_select",
        )(iq, iwt, ki_all)
    kern = functools.partial(_select_kernel, tq=tq, tk=tk, n_tiles=s_pad // tk, past=past, s_valid=s_valid,
                             topk=topk)
    return pl.pallas_call(
        kern,
        grid=(b, nq),
        in_specs=[pl.BlockSpec((IDX_HEADS, tq, LANES), lambda bi, i: (0, bi * nq + i, 0)),
                  pl.BlockSpec((tq, LANES), lambda bi, i: (bi * nq + i, 0)),
                  pl.BlockSpec((None, s_pad, LANES), lambda bi, i: (bi, 0, 0))],
        out_specs=pl.BlockSpec((None, tq, s_pad), lambda bi, i: (bi, i, 0)),
        out_shape=jax.ShapeDtypeStruct((b, t, s_pad), MASK_DTYPE),
        scratch_shapes=[pltpu.VMEM((tq, s_pad), I32), pltpu.VMEM((tq, LANES), I32), pltpu.VMEM((tq, LANES), I32)],
        compiler_params=_cparams(("parallel", "parallel")),
        name="dsa_select",
    )(iq, iw, ki_all)


def _attn_kernel(*refs, tq, tk, hpg, dqk, dv, past, s_valid, scale, has_mask, kv_shared):
    if has_mask:
        q_ref, k_ref, v_ref, mask_ref, o_ref, m_ref, acc_ref, sa_ref, sb_ref = refs
    else:
        q_ref, k_ref, v_ref, o_ref, m_ref, acc_ref, sa_ref, sb_ref, vis_ref = refs
    i = pl.program_id(2)
    nk = _visible_tiles(i, tq=tq, tk=tk, past=past, s_valid=s_valid)
    m_ref[...] = jnp.full(m_ref.shape, NEG_INF, F32)
    acc_ref[...] = jnp.zeros(acc_ref.shape, F32)
    lane_k = lax.broadcasted_iota(I32, (tq, tk), 1)
    if not has_mask:
        qchunk = (past + i * tq + lax.broadcasted_iota(I32, (tq, tk), 0)) >> CHUNK_SHIFT
        vis_ref[...] = qchunk - (lane_k >> CHUNK_SHIFT)
    c = scale * LOG2_E
    ones_col = jnp.where(lax.broadcasted_iota(I32, (tk, LANES), 1) == 0, 1.0, 0.0).astype(BF16)

    def scores(j, s_ref):
        rows = pl.ds(j * tk, tk)
        for h in range(hpg):
            hk = 0 if kv_shared else h
            kt = k_ref[rows, hk * dqk:(hk + 1) * dqk]
            q = q_ref[:, h * dqk:(h + 1) * dqk]
            s = lax.dot_general(q, kt, (((1,), (1,)), ((), ())), preferred_element_type=F32)
            s_ref[h] = s * c

    def softmax_pv(j, s_ref, masked=True):
        rows = pl.ds(j * tk, tk)
        if not masked:
            keep = None
        elif has_mask:
            keep = mask_ref[:, rows].astype(I32) != 0
        else:
            keep = vis_ref[...] >= ((j * tk) >> CHUNK_SHIFT)
            if s_valid % tk:
                keep = keep & (j * tk + lane_k < s_valid)
        for h in range(hpg):
            hk = 0 if kv_shared else h
            v1 = jnp.concatenate([v_ref[rows, hk * dv:(hk + 1) * dv], ones_col], axis=1)
            s = jnp.where(keep, s_ref[h], NEG_INF) if masked else s_ref[h]
            m_prev = m_ref[h]
            m_new = jnp.maximum(m_prev, jnp.max(s, axis=1, keepdims=True))
            alpha = jnp.exp2(m_prev - m_new)
            p = jnp.exp2(s - jnp.tile(m_new, (1, tk // LANES)))
            pv = jnp.dot(p.astype(BF16), v1, preferred_element_type=F32)
            acc_ref[h] = jnp.tile(alpha, (1, 2)) * acc_ref[h] + pv
            m_ref[h] = m_new

    def pair_body(jj, carry, masked):
        j0 = 2 * jj
        scores(j0 + 1, sb_ref)
        softmax_pv(j0, sa_ref, masked)
        scores(j0 + 2, sa_ref)
        softmax_pv(j0 + 1, sb_ref, masked)
        return carry

    scores(0, sa_ref)
    n_pairs = (nk - 1) // 2
    if has_mask:
        n_free = 0
    else:
        first_chunk_end = (((past + i * tq) >> CHUNK_SHIFT) + 1) << CHUNK_SHIFT
        n_free = jnp.minimum(jnp.minimum(first_chunk_end, s_valid) // (2 * tk), n_pairs)
        lax.fori_loop(0, n_free, functools.partial(pair_body, masked=False), 0)
    lax.fori_loop(n_free, n_pairs, functools.partial(pair_body, masked=True), 0)
    j0 = 2 * n_pairs

    @pl.when(nk - j0 == 2)
    def _():
        scores(j0 + 1, sb_ref)
        softmax_pv(j0, sa_ref)
        softmax_pv(j0 + 1, sb_ref)

    @pl.when(nk - j0 == 1)
    def _():
        softmax_pv(j0, sa_ref)

    for h in range(hpg):
        acc = acc_ref[h]
        o_ref[:, h * dv:(h + 1) * dv] = (acc[:, :dv] / acc[:, dv:dv + 1]).astype(o_ref.dtype)


def _attention(q, k, v, mask, *, groups, hpg, dqk, kv_shared, past, s_valid, tk, scale):
    b, t, _ = q.shape
    s_pad = k.shape[1]
    dv = LANES
    kvh = 1 if kv_shared else hpg
    tq = _row_tile(t, 1024)
    kern = functools.partial(_attn_kernel, tq=tq, tk=tk, hpg=hpg, dqk=dqk, dv=dv, past=past, s_valid=s_valid,
                             scale=scale, has_mask=mask is not None, kv_shared=kv_shared)
    in_specs = [pl.BlockSpec((None, tq, hpg * dqk), lambda bi, g, i: (bi, i, g)),
                pl.BlockSpec((None, s_pad, kvh * dqk), lambda bi, g, i: (bi, 0, g)),
                pl.BlockSpec((None, s_pad, kvh * dv), lambda bi, g, i: (bi, 0, g))]
    args = [q, k, v]
    if mask is not None:
        in_specs.append(pl.BlockSpec((None, tq, s_pad), lambda bi, g, i: (bi, i, 0)))
        args.append(mask)
    return pl.pallas_call(
        kern,
        grid=(b, groups, t // tq),
        in_specs=in_specs,
        out_specs=pl.BlockSpec((None, tq, hpg * dv), lambda bi, g, i: (bi, i, g)),
        out_shape=jax.ShapeDtypeStruct((b, t, groups * hpg * dv), BF16),
        scratch_shapes=[pltpu.VMEM((hpg, tq, LANES), F32), pltpu.VMEM((hpg, tq, 2 * dv), F32),
                        pltpu.VMEM((hpg, tq, tk), F32), pltpu.VMEM((hpg, tq, tk), F32)]
        + ([] if mask is not None else [pltpu.VMEM((tq, tk), I32)]),
        compiler_params=_cparams(("parallel", "parallel", "arbitrary")),
        name="dsa_attention" if mask is not None else "mla_attention",
    )(*args)


def _merge_kernel(x_ref, oa_ref, ob_ref, oc_ref, g0_ref, g1_ref, g2_ref, bg_ref, wa_ref, wb_ref, wc_ref, wo_ref,
                  o_ref):
    merged = None
    for n, (o_r, g_r, w_r) in enumerate(((oa_ref, g0_ref, wa_ref), (ob_ref, g1_ref, wb_ref),
                                         (oc_ref, g2_ref, wc_ref))):
        gate = _sigmoid(g_r[...] + bg_ref[:, n * D_MODEL:(n + 1) * D_MODEL])
        term = gate * jnp.dot(o_r[...], w_r[...], preferred_element_type=F32)
        merged = term if merged is None else merged + term
    o_ref[...] = x_ref[...] + jnp.dot(merged.astype(BF16), wo_ref[...], preferred_element_type=F32)


def _merge(x, oa, ob, oc, z, bg, wa, wb, wc, wo):
    m, d = x.shape
    tm = _row_tile(m, 512)
    rows = pl.BlockSpec((tm, d), lambda i: (i, 0))
    gblk = lambda n: pl.BlockSpec((tm, d), lambda i: (i, Z_G // d + n))
    full = lambda shape: pl.BlockSpec(shape, lambda i: (0,) * len(shape))
    return pl.pallas_call(
        _merge_kernel,
        grid=(m // tm,),
        in_specs=[rows, rows, rows, rows, gblk(0), gblk(1), gblk(2), full((1, 3 * d)),
                  full((d, d)), full((d, d)), full((d, d)), full((d, d))],
        out_specs=rows,
        out_shape=jax.ShapeDtypeStruct((m, d), F32),
        compiler_params=_cparams(("parallel",)),
        name="merge",
    )(x, oa, ob, oc, z, z, z, bg.reshape(1, -1), wa, wb, wc, wo)


def _ffn_kernel(*refs, final):
    if final:
        x_ref, g_ref, wg_ref, wu_ref, wd_ref, gf_ref, o_ref, hn_ref, acc_ref = refs
    else:
        x_ref, g_ref, wg_ref, wu_ref, wd_ref, o_ref, hn_ref, acc_ref = refs
    j = pl.program_id(1)

    @pl.when(j == 0)
    def _():
        hn_ref[...] = _rms(x_ref[...], g_ref[...]).astype(BF16)
        acc_ref[...] = jnp.zeros(acc_ref.shape, F32)

    hn = hn_ref[...]
    gt = jnp.dot(hn, wg_ref[...], preferred_element_type=F32)
    up = jnp.dot(hn, wu_ref[...], preferred_element_type=F32)
    act = (gt * _sigmoid(gt) * up).astype(BF16)
    acc_ref[...] += jnp.dot(act, wd_ref[...], preferred_element_type=F32)

    @pl.when(j == pl.num_programs(1) - 1)
    def _():
        y = x_ref[...] + acc_ref[...]
        o_ref[...] = _rms(y, gf_ref[...]) if final else y


def _ffn(x, g, wg, wu, wd, gf=None):
    m, d = x.shape
    f = wg.shape[1]
    tm, tf = _row_tile(m, 1024), f // 2
    final = gf is not None
    in_specs = [pl.BlockSpec((tm, d), lambda i, j: (i, 0)), pl.BlockSpec((1, d), lambda i, j: (0, 0)),
                pl.BlockSpec((d, tf), lambda i, j: (0, j)), pl.BlockSpec((d, tf), lambda i, j: (0, j)),
                pl.BlockSpec((tf, d), lambda i, j: (j, 0))]
    args = [x, g.reshape(1, d), wg, wu, wd]
    if final:
        in_specs.append(pl.BlockSpec((1, d), lambda i, j: (0, 0)))
        args.append(gf.reshape(1, d))
    return pl.pallas_call(
        functools.partial(_ffn_kernel, final=final),
        grid=(m // tm, f // tf),
        in_specs=in_specs,
        out_specs=pl.BlockSpec((tm, d), lambda i, j: (i, 0)),
        out_shape=jax.ShapeDtypeStruct((m, d), F32),
        scratch_shapes=[pltpu.VMEM((tm, d), BF16), pltpu.VMEM((tm, d), F32)],
        compiler_params=_cparams(("parallel", "arbitrary")),
        name="ffn",
    )(*args)


def _pad_cols(w, width):
    return jnp.concatenate([w, jnp.zeros(w.shape[:-1] + (width - w.shape[-1],), w.dtype)], axis=-1)


def _layout_w_in(w):
    a_gate, a_x, b_q, b_kv, b_kr = w[:, 0:1024], w[:, 1024:2048], w[:, 2048:2432], w[:, 2432:2688], w[:, 2688:2752]
    c_q, c_k, c_v, c_iq = w[:, 2752:3776], w[:, 3776:4288], w[:, 4288:4800], w[:, 4800:5824]
    c_ik, c_iw, g = w[:, 5824:5888], w[:, 5888:5904], w[:, 5904:8976]
    out = jnp.concatenate([a_gate, a_x, c_q, c_iq, g, c_k, c_v, b_kv, _pad_cols(b_kr, LANES),
                           _pad_cols(c_ik, LANES), _pad_cols(c_iw, LANES), b_q], axis=1)
    assert out.shape[1] == Z_WIDTH
    return out.astype(BF16)


def _layout_mla(w_q_up, w_kv_up):
    r = w_q_up.shape[0]
    zeros = jnp.zeros((r, MLA_HEADS, MLA_QK_PAD - MLA_NOPE - MLA_ROPE), w_q_up.dtype)
    wq = jnp.concatenate([w_q_up, zeros], axis=-1).reshape(r, MLA_HEADS * MLA_QK_PAD).astype(BF16)
    wk = w_kv_up[:, :, :MLA_NOPE].reshape(MLA_KV_LORA, MLA_HEADS * MLA_NOPE).astype(BF16)
    wv = w_kv_up[:, :, MLA_NOPE:].reshape(MLA_KV_LORA, MLA_HEADS * MLA_V).astype(BF16)
    return wq, wk, wv


def _key_tile(s_valid):
    return 512 if s_valid % 512 == 0 else 384


def _pad_keys(x, s_pad):
    b, s = x.shape[:2]
    if s == s_pad:
        return x
    return jnp.concatenate([x, jnp.zeros((b, s_pad - s) + x.shape[2:], x.dtype)], axis=1)


def _layer(x3, pos, past, w, final_g):
    (attn_norm_g, w_in, b_gates, lru_conv_w, lru_conv_b, lru_wa, lru_ba, lru_wx, lru_bx, lru_lambda,
     mla_q_norm_g, mla_kv_norm_g, mla_w_q_up, mla_w_kv_up, idx_k_norm_g, idx_k_norm_b,
     w_branch_a, w_branch_b, w_branch_c, w_out, ffn_norm_g, w_ffn_gate, w_ffn_up, w_ffn_down) = w
    past_lat, past_kr, past_k, past_v, past_ki, conv_buf, h0 = past
    b, t, d = x3.shape
    m = b * t
    past_len = 0 if past_lat is None else past_lat.shape[1]
    s_valid = past_len + t
    tk = _key_tile(s_valid)
    s_pad = -(-s_valid // tk) * tk
    x = x3.reshape(m, d)

    z = _norm_matmul(x, attn_norm_g, _layout_w_in(w_in))
    z3 = z.reshape(b, t, Z_WIDTH)

    o_a, conv8, h8 = _lru(z3, conv_buf, h0, lru_conv_w, lru_conv_b, lru_wa, lru_ba, lru_wx, lru_bx,
                          lru_lambda.reshape(-1))
    conv_new, h_new = conv8[:, 8 - (CONV_W - 1):], h8[:, 7]

    wq, wk, wv = _layout_mla(mla_w_q_up, mla_w_kv_up)
    cos64, sin64 = _rope_tables(pos, MLA_ROPE, MLA_ROPE)
    q_b, lat_new, kr_pad = _mla_q(z, cos64, sin64, mla_q_norm_g, mla_kv_norm_g, wq)
    lat_new = lat_new.reshape(b, t, MLA_KV_LORA)
    kr_pad = kr_pad.reshape(b, t, LANES)
    kr_new = kr_pad[:, :, :MLA_ROPE]
    if past_lat is None:
        lat_all, kr_all = lat_new, kr_pad
    else:
        lat_all = jnp.concatenate([past_lat.astype(F32), lat_new], axis=1)
        kr_all = jnp.concatenate([_pad_cols(past_kr.astype(F32), LANES), kr_pad], axis=1)
    lat_all, kr_all = _pad_keys(lat_all, s_pad), _pad_keys(kr_all, s_pad)
    k_b, v_b = _mla_kv(lat_all.reshape(b * s_pad, -1), kr_all.reshape(b * s_pad, -1), wk, wv)
    o_b = _attention(q_b.reshape(b, t, -1), k_b.reshape(b, s_pad, -1), v_b.reshape(b, s_pad, -1), None,
                     groups=MLA_HEADS // 2, hpg=2, dqk=MLA_QK_PAD, kv_shared=False, past=past_len,
                     s_valid=s_valid, tk=tk, scale=(MLA_NOPE + MLA_ROPE) ** -0.5)

    tabs = _rope_tables(pos, DSA_HD, DSA_HD) + _rope_tables(pos, IDX_ROPE, IDX_HD)
    q_c, k_new, iq, ki_pad, iw = _dsa_prep(z, tabs, idx_k_norm_g, idx_k_norm_b)
    v_new = z3[:, :, Z_C_V:Z_C_V + DSA_KV_HEADS * DSA_HD]
    k_new = k_new.reshape(b, t, -1)
    ki_pad = ki_pad.reshape(b, t, LANES)
    ki_new = ki_pad[:, :, :IDX_HD]
    if past_k is None:
        k_all, v_all, ki_all = k_new, v_new, ki_pad
    else:
        k_all = jnp.concatenate([past_k.astype(F32).reshape(b, past_len, -1), k_new], axis=1)
        v_all = jnp.concatenate([past_v.astype(F32).reshape(b, past_len, -1), v_new], axis=1)
        ki_all = jnp.concatenate([_pad_cols(past_ki.astype(F32), LANES), ki_pad], axis=1)
    k_all, v_all, ki_all = (_pad_keys(a.astype(BF16), s_pad) for a in (k_all, v_all, ki_all))
    keep = _dsa_select(iq, iw, ki_all, b=b, t=t, past=past_len, s_valid=s_valid, tk=tk)
    o_c = _attention(q_c.reshape(b, t, -1), k_all, v_all, keep, groups=DSA_KV_HEADS,
                     hpg=DSA_HEADS // DSA_KV_HEADS, dqk=DSA_HD, kv_shared=True, past=past_len, s_valid=s_valid,
                     tk=tk, scale=DSA_HD ** -0.5)

    bf = lambda a: a.astype(BF16)
    x = _merge(x, o_a.reshape(m, -1), o_b.reshape(m, -1), o_c.reshape(m, -1), z, b_gates,
               bf(w_branch_a), bf(w_branch_b), bf(w_branch_c), bf(w_out))
    x = _ffn(x, ffn_norm_g, bf(w_ffn_gate), bf(w_ffn_up), bf(w_ffn_down), final_g)
    new = (lat_new, kr_new, k_new.reshape(b, t, DSA_KV_HEADS, DSA_HD), v_new.reshape(b, t, DSA_KV_HEADS, DSA_HD),
           ki_new, conv_new, h_new)
    return x.reshape(b, t, d), new


def _trunk(x, caches, weights, final_norm_g):
    b, t, _ = x.shape
    depth = weights[0].shape[0]
    past_len = 0 if caches is None else caches[0].shape[2]
    pos = past_len + jnp.arange(t, dtype=I32)
    new = []
    for l in range(depth):
        w_l = tuple(wt[l] for wt in weights)
        if caches is None:
            past = (None, None, None, None, None, jnp.zeros((b, CONV_W - 1, LRU_WIDTH), F32),
                    jnp.zeros((b, LRU_WIDTH), F32))
        else:
            past = tuple(c[l] for c in caches)
        x, st = _layer(x, pos, past, w_l, final_norm_g if l == depth - 1 else None)
        new.append(st)
    return x, tuple(jnp.stack([s[i] for s in new]) for i in range(7))


def kernel(x_prompt, x_sample, cache_mla_latent, cache_mla_krope, cache_dsa_k, cache_dsa_v, cache_dsa_kidx,
           state_lru_conv, state_lru_h, attn_norm_g, w_in, b_gates, lru_conv_w, lru_conv_b, lru_wa, lru_ba,
           lru_wx, lru_bx, lru_lambda, mla_q_norm_g, mla_kv_norm_g, mla_w_q_up, mla_w_kv_up, idx_k_norm_g,
           idx_k_norm_b, w_branch_a, w_branch_b, w_branch_c, w_out, ffn_norm_g, w_ffn_gate, w_ffn_up,
           w_ffn_down, final_norm_g):
    weights = (attn_norm_g, w_in, b_gates, lru_conv_w, lru_conv_b, lru_wa, lru_ba, lru_wx, lru_bx, lru_lambda,
               mla_q_norm_g, mla_kv_norm_g, mla_w_q_up, mla_w_kv_up, idx_k_norm_g, idx_k_norm_b,
               w_branch_a, w_branch_b, w_branch_c, w_out, ffn_norm_g, w_ffn_gate, w_ffn_up, w_ffn_down)
    y_p, st_p = _trunk(x_prompt, None, weights, final_norm_g)
    caches = (cache_mla_latent, cache_mla_krope, cache_dsa_k, cache_dsa_v, cache_dsa_kidx, state_lru_conv,
              state_lru_h)
    y_s, st_s = _trunk(x_sample, caches, weights, final_norm_g)
    (lat_p, kr_p, k_p, v_p, ki_p, conv_p, h_p) = st_p
    (lat_s, kr_s, k_s, v_s, ki_s, conv_s, h_s) = st_s
    return (y_p, y_s, lat_p, lat_s, kr_p, kr_s, k_p, k_s, v_p, v_s, ki_p, ki_s, conv_p, conv_s, h_p, h_s)
```

```python
import functools

import jax
import jax.numpy as jnp
from jax import lax
from jax.experimental import pallas as pl
from jax.experimental.pallas import tpu as pltpu

F32 = jnp.float32
BF16 = jnp.bfloat16
I32 = jnp.int32

D_MODEL = 1024
CHUNK_SHIFT = 6
ROPE_THETA = 10000.0
NORM_EPS = 1e-6
NEG_INF = -1e30
INT_MIN = -(2 ** 31)
LOG2_E = 1.4426950408889634
INDEX_BITS = 14
WORD_KEYS = 256
MASK_DTYPE = jnp.int8

LRU_WIDTH = 1024
LRU_BLOCKS = 8
LRU_BLOCK = 128
CONV_W = 4
LRU_C = 8.0

MLA_HEADS = 8
MLA_Q_LORA = 384
MLA_KV_LORA = 256
MLA_NOPE = 128
MLA_ROPE = 64
MLA_V = 128
MLA_QK_PAD = 256

DSA_HEADS = 8
DSA_KV_HEADS = 4
DSA_HD = 128
IDX_HEADS = 16
IDX_HD = 64
IDX_ROPE = 32
TOPK_MAX = 256
D_FF = 2816

LANES = 128
VMEM_LIMIT = 56 * 1024 * 1024

Z_A_GATE, Z_A_X, Z_C_Q, Z_C_IQ, Z_G = 0, 1024, 2048, 3072, 4096
Z_C_K, Z_C_V, Z_B_KV, Z_B_KR, Z_C_IK, Z_C_IW, Z_B_Q = 7168, 7680, 8192, 8448, 8576, 8704, 8832
Z_WIDTH = 9216


def _cparams(sem):
    return pltpu.CompilerParams(dimension_semantics=sem, vmem_limit_bytes=VMEM_LIMIT)


def _sigmoid(x):
    return 1.0 / (1.0 + jnp.exp(-x))


def _gelu_tanh(x):
    return 0.5 * x * (1.0 + jnp.tanh(0.7978845608028654 * (x + 0.044715 * (x * x * x))))


def _rms(x, g):
    return x * lax.rsqrt(jnp.mean(x * x, axis=-1, keepdims=True) + NORM_EPS) * g


def _row_tile(m, pref):
    t = min(m, pref)
    assert m % t == 0, (m, t)
    return t


def _norm_matmul_kernel(x_ref, g_ref, w_ref, o_ref, xn_ref):
    @pl.when(pl.program_id(1) == 0)
    def _():
        xn_ref[...] = _rms(x_ref[...], g_ref[...]).astype(BF16)

    o_ref[...] = jnp.dot(xn_ref[...], w_ref[...], preferred_element_type=F32)


def _norm_matmul(x, g, w):
    m, d = x.shape
    n = w.shape[1]
    tm, tn = _row_tile(m, 2048), 1536
    assert n % tn == 0
    return pl.pallas_call(
        _norm_matmul_kernel,
        grid=(m // tm, n // tn),
        in_specs=[pl.BlockSpec((tm, d), lambda i, j: (i, 0), pipeline_mode=pl.Buffered(1)),
                  pl.BlockSpec((1, d), lambda i, j: (0, 0)),
                  pl.BlockSpec((d, tn), lambda i, j: (0, j))],
        out_specs=pl.BlockSpec((tm, tn), lambda i, j: (i, j)),
        out_shape=jax.ShapeDtypeStruct((m, n), F32),
        scratch_shapes=[pltpu.VMEM((tm, d), BF16)],
        compiler_params=_cparams(("parallel", "arbitrary")),
        name="norm_matmul",
    )(x, g.reshape(1, d), w)


def _lru_kernel(gate_ref, xin_ref, cbuf_ref, h0_ref, cw_ref, cb_ref, wa_ref, ba_ref, wx_ref, bx_ref, lam_ref,
                o_ref, clast_ref, hlast_ref, prev_ref, hc_ref, *, tt):
    @pl.when(pl.program_id(1) == 0)
    def _():
        prev_ref[...] = cbuf_ref[...]
        hc_ref[...] = h0_ref[...]

    row = lax.broadcasted_iota(I32, (tt, LRU_BLOCK), 0)
    row8 = lax.broadcasted_iota(I32, (8, LRU_BLOCK), 0)
    for n in range(LRU_BLOCKS):
        sl = slice(n * LRU_BLOCK, (n + 1) * LRU_BLOCK)
        x = xin_ref[:, sl]
        prev = prev_ref[:, sl]
        u = cb_ref[:, sl]
        for j in range(CONV_W):
            d = CONV_W - 1 - j
            if d == 0:
                xs = x
            else:
                rx = pltpu.roll(x, d, axis=0)
                head = jnp.where(row8 < d, pltpu.roll(prev, d, axis=0), rx[:8])
                xs = head if tt == 8 else jnp.concatenate([head, rx[8:]], axis=0)
            u = u + xs * cw_ref[j:j + 1, sl]
        ub = u.astype(BF16)
        r = _sigmoid(jnp.dot(ub, wa_ref[n], preferred_element_type=F32) + ba_ref[:, sl])
        ig = _sigmoid(jnp.dot(ub, wx_ref[n], preferred_element_type=F32) + bx_ref[:, sl])
        lam = lam_ref[:, sl]
        log_sig = jnp.minimum(lam, 0.0) - jnp.log1p(jnp.exp(-jnp.abs(lam)))
        a = jnp.exp(LRU_C * r * log_sig)
        b = jnp.sqrt(1.0 - a * a) * (ig * u)
        d = 1
        while d < tt:
            if d < 8:
                keep = row >= d
                b = jnp.where(keep, a * pltpu.roll(b, d, axis=0) + b, b)
                a = jnp.where(keep, a * pltpu.roll(a, d, axis=0), a)
            else:
                b = jnp.concatenate([b[:d], a[d:] * b[:tt - d] + b[d:]], axis=0)
                a = jnp.concatenate([a[:d], a[d:] * a[:tt - d]], axis=0)
            d *= 2
        h = a * hc_ref[:, sl] + b
        hc_ref[:, sl] = h[tt - 1:tt]
        o_ref[:, sl] = (h * _gelu_tanh(gate_ref[:, sl])).astype(o_ref.dtype)
        hlast_ref[:, sl] = h[tt - 8:]
        clast_ref[:, sl] = x[tt - 8:]
        prev_ref[:, sl] = x[tt - 8:]


def _lru(z3, conv_buf, h0, cw, cb, wa, ba, wx, bx, lam):
    b, t, _ = z3.shape
    w = LRU_WIDTH
    tt = _row_tile(t, 256)
    assert tt % 8 == 0 and tt & (tt - 1) == 0
    cbuf8 = jnp.concatenate([jnp.zeros((b, 8 - (CONV_W - 1), w), F32), conv_buf.astype(F32)], axis=1)
    cw8 = jnp.concatenate([cw, jnp.zeros((8 - CONV_W, w), F32)], axis=0)
    row = lambda v: v.reshape(1, w)
    full = lambda shape: pl.BlockSpec(shape, lambda bi, i: (0,) * len(shape))
    return pl.pallas_call(
        functools.partial(_lru_kernel, tt=tt),
        grid=(b, t // tt),
        in_specs=[pl.BlockSpec((None, tt, w), lambda bi, i: (bi, i, Z_A_GATE // w)),
                  pl.BlockSpec((None, tt, w), lambda bi, i: (bi, i, Z_A_X // w)),
                  pl.BlockSpec((None, 8, w), lambda bi, i: (bi, 0, 0)),
                  pl.BlockSpec((None, 1, w), lambda bi, i: (bi, 0, 0)),
                  full((8, w)), full((1, w)),
                  full((LRU_BLOCKS, LRU_BLOCK, LRU_BLOCK)), full((1, w)),
                  full((LRU_BLOCKS, LRU_BLOCK, LRU_BLOCK)), full((1, w)), full((1, w))],
        out_specs=[pl.BlockSpec((None, tt, w), lambda bi, i: (bi, i, 0)),
                   pl.BlockSpec((None, 8, w), lambda bi, i: (bi, 0, 0)),
                   pl.BlockSpec((None, 8, w), lambda bi, i: (bi, 0, 0))],
        out_shape=[jax.ShapeDtypeStruct((b, t, w), BF16),
                   jax.ShapeDtypeStruct((b, 8, w), F32),
                   jax.ShapeDtypeStruct((b, 8, w), F32)],
        scratch_shapes=[pltpu.VMEM((8, w), F32), pltpu.VMEM((1, w), F32)],
        compiler_params=_cparams(("parallel", "arbitrary")),
        name="lru",
    )(z3, z3, cbuf8, h0.astype(F32).reshape(b, 1, w), cw8, row(cb), wa.astype(BF16), row(ba),
      wx.astype(BF16), row(bx), row(lam))


def _rope_tables(pos, d, period, width=LANES):
    half = d // 2
    inv = ROPE_THETA ** (-jnp.arange(0, d, 2, dtype=F32) / d)
    ang = pos.astype(F32)[:, None] * inv[None, :]
    cos, sin = jnp.cos(ang), jnp.sin(ang)
    t = pos.shape[0]
    cos_p = jnp.concatenate([cos, cos, jnp.ones((t, period - d), F32)], axis=1)
    sin_p = jnp.concatenate([-sin, sin, jnp.zeros((t, period - d), F32)], axis=1)
    reps = width // period
    assert half * 2 == d and reps * period == width
    return jnp.tile(cos_p, (1, reps)), jnp.tile(sin_p, (1, reps))


def _rotate(x, cos_t, sin_t, half, period):
    if 2 * half == LANES:
        partner = pltpu.roll(x, half, axis=1)
    else:
        lane = lax.broadcasted_iota(I32, x.shape, 1)
        partner = jnp.where((lane & (period - 1)) < half,
                            pltpu.roll(x, LANES - half, axis=1), pltpu.roll(x, half, axis=1))
    return x * cos_t + partner * sin_t


def _mla_q_kernel(zq_ref, zkv_ref, zkr_ref, cos_ref, sin_ref, qg_ref, kvg_ref, wq_ref, q_ref, lat_ref, kr_ref):
    cos_t, sin_t = cos_ref[...], sin_ref[...]
    qn = _rms(zq_ref[...], qg_ref[...]).astype(BF16)
    q = jnp.dot(qn, wq_ref[...], preferred_element_type=F32)
    for h in range(MLA_HEADS):
        c0 = h * MLA_QK_PAD
        q_ref[:, c0:c0 + LANES] = q[:, c0:c0 + LANES].astype(BF16)
        q_ref[:, c0 + LANES:c0 + 2 * LANES] = _rotate(
            q[:, c0 + LANES:c0 + 2 * LANES], cos_t, sin_t, MLA_ROPE // 2, MLA_ROPE).astype(BF16)
    lat_ref[...] = _rms(zkv_ref[...], kvg_ref[...])
    kr_ref[...] = _rotate(zkr_ref[...], cos_t, sin_t, MLA_ROPE // 2, MLA_ROPE)


def _mla_q(z, cos_t, sin_t, qg, kvg, wq):
    m = z.shape[0]
    t = cos_t.shape[0]
    tm = _row_tile(t, 512)
    nt = t // tm
    zblk = lambda width, off: pl.BlockSpec((tm, width), lambda i: (i, off // width))
    tab = pl.BlockSpec((tm, LANES), lambda i: (i % nt, 0))
    full = lambda shape: pl.BlockSpec(shape, lambda i: (0,) * len(shape))
    return pl.pallas_call(
        _mla_q_kernel,
        grid=(m // tm,),
        in_specs=[zblk(MLA_Q_LORA, Z_B_Q), zblk(MLA_KV_LORA, Z_B_KV), zblk(LANES, Z_B_KR), tab, tab,
                  full((1, MLA_Q_LORA)), full((1, MLA_KV_LORA)), full((MLA_Q_LORA, MLA_HEADS * MLA_QK_PAD))],
        out_specs=[pl.BlockSpec((tm, MLA_HEADS * MLA_QK_PAD), lambda i: (i, 0)),
                   pl.BlockSpec((tm, MLA_KV_LORA), lambda i: (i, 0)),
                   pl.BlockSpec((tm, LANES), lambda i: (i, 0))],
        out_shape=[jax.ShapeDtypeStruct((m, MLA_HEADS * MLA_QK_PAD), BF16),
                   jax.ShapeDtypeStruct((m, MLA_KV_LORA), F32),
                   jax.ShapeDtypeStruct((m, LANES), F32)],
        compiler_params=_cparams(("parallel",)),
        name="mla_q",
    )(z, z, z, cos_t, sin_t, qg.reshape(1, -1), kvg.reshape(1, -1), wq)


def _mla_kv_kernel(lat_ref, kr_ref, wk_ref, wv_ref, k_ref, v_ref):
    latb = lat_ref[...].astype(BF16)
    k = jnp.dot(latb, wk_ref[...], preferred_element_type=F32)
    v_ref[...] = jnp.dot(latb, wv_ref[...], preferred_element_type=F32).astype(BF16)
    krb = kr_ref[...].astype(BF16)
    for h in range(MLA_HEADS):
        k_ref[:, h * MLA_QK_PAD:h * MLA_QK_PAD + LANES] = k[:, h * MLA_NOPE:(h + 1) * MLA_NOPE].astype(BF16)
        k_ref[:, h * MLA_QK_PAD + LANES:(h + 1) * MLA_QK_PAD] = krb


def _mla_kv(lat_all, kr_all, wk, wv):
    m = lat_all.shape[0]
    tm = 384 if m % 512 else 512
    assert m % tm == 0
    full = lambda shape: pl.BlockSpec(shape, lambda i: (0,) * len(shape))
    return pl.pallas_call(
        _mla_kv_kernel,
        grid=(m // tm,),
        in_specs=[pl.BlockSpec((tm, MLA_KV_LORA), lambda i: (i, 0)), pl.BlockSpec((tm, LANES), lambda i: (i, 0)),
                  full(wk.shape), full(wv.shape)],
        out_specs=[pl.BlockSpec((tm, MLA_HEADS * MLA_QK_PAD), lambda i: (i, 0)),
                   pl.BlockSpec((tm, MLA_HEADS * MLA_V), lambda i: (i, 0))],
        out_shape=[jax.ShapeDtypeStruct((m, MLA_HEADS * MLA_QK_PAD), BF16),
                   jax.ShapeDtypeStruct((m, MLA_HEADS * MLA_V), BF16)],
        compiler_params=_cparams(("parallel",)),
        name="mla_kv",
    )(lat_all, kr_all, wk, wv)


def _dsa_prep_kernel(zq_ref, zk_ref, ziq_ref, zik_ref, ziw_ref, cos_ref, sin_ref, cosp_ref, sinp_ref,
                     ikg_ref, ikb_ref, q_ref, k_ref, iq_ref, ki_ref, iw_ref):
    cos_t, sin_t = cos_ref[...], sin_ref[...]
    cos_p, sin_p = cosp_ref[...], sinp_ref[...]
    for h in range(DSA_HEADS):
        sl = slice(h * DSA_HD, (h + 1) * DSA_HD)
        q_ref[:, sl] = _rotate(zq_ref[:, sl], cos_t, sin_t, DSA_HD // 2, DSA_HD).astype(BF16)
    for h in range(DSA_KV_HEADS):
        sl = slice(h * DSA_HD, (h + 1) * DSA_HD)
        k_ref[:, sl] = _rotate(zk_ref[:, sl], cos_t, sin_t, DSA_HD // 2, DSA_HD)
    lane = lax.broadcasted_iota(I32, cos_t.shape, 1)
    low = lane < IDX_HD
    for c in range(IDX_HEADS // 2):
        y = _rotate(ziq_ref[:, c * LANES:(c + 1) * LANES], cos_p, sin_p, IDX_ROPE // 2, IDX_HD)
        iq_ref[2 * c] = jnp.where(low, y, 0.0).astype(BF16)
        iq_ref[2 * c + 1] = jnp.where(low, pltpu.roll(y, IDX_HD, axis=1), 0.0).astype(BF16)
    x = zik_ref[...]
    mean = jnp.sum(x, axis=-1, keepdims=True) * (1.0 / IDX_HD)
    xc = jnp.where(low, x - mean, 0.0)
    var = jnp.sum(xc * xc, axis=-1, keepdims=True) * (1.0 / IDX_HD)
    y = xc * lax.rsqrt(var + NORM_EPS) * ikg_ref[...] + ikb_ref[...]
    ki_ref[...] = _rotate(y, cos_p, sin_p, IDX_ROPE // 2, IDX_HD)
    iw_ref[...] = ziw_ref[...] * ((IDX_HEADS * IDX_HD) ** -0.5)


def _dsa_prep(z, tabs, ikg, ikb):
    m = z.shape[0]
    cos_t, sin_t, cos_p, sin_p = tabs
    t = cos_t.shape[0]
    tm = _row_tile(t, 512)
    nt = t // tm
    zblk = lambda width, off: pl.BlockSpec((tm, width), lambda i: (i, off // width))
    tab = pl.BlockSpec((tm, LANES), lambda i: (i % nt, 0))
    full = lambda shape: pl.BlockSpec(shape, lambda i: (0,) * len(shape))
    pad = lambda v: jnp.concatenate([v, jnp.zeros((LANES - IDX_HD,), F32)]).reshape(1, LANES)
    return pl.pallas_call(
        _dsa_prep_kernel,
        grid=(m // tm,),
        in_specs=[zblk(1024, Z_C_Q), zblk(512, Z_C_K), zblk(1024, Z_C_IQ), zblk(LANES, Z_C_IK), zblk(LANES, Z_C_IW),
                  tab, tab, tab, tab, full((1, LANES)), full((1, LANES))],
        out_specs=[pl.BlockSpec((tm, 1024), lambda i: (i, 0)),
                   pl.BlockSpec((tm, 512), lambda i: (i, 0)),
                   pl.BlockSpec((IDX_HEADS, tm, LANES), lambda i: (0, i, 0)),
                   pl.BlockSpec((tm, LANES), lambda i: (i, 0)),
                   pl.BlockSpec((tm, LANES), lambda i: (i, 0))],
        out_shape=[jax.ShapeDtypeStruct((m, 1024), BF16),
                   jax.ShapeDtypeStruct((m, 512), F32),
                   jax.ShapeDtypeStruct((IDX_HEADS, m, LANES), BF16),
                   jax.ShapeDtypeStruct((m, LANES), F32),
                   jax.ShapeDtypeStruct((m, LANES), F32)],
        compiler_params=_cparams(("parallel",)),
        name="dsa_prep",
    )(z, z, z, z, z, cos_t, sin_t, cos_p, sin_p, pad(ikg), pad(ikb))


def _visible_tiles(i, *, tq, tk, past, s_valid):
    last = past + (i + 1) * tq - 1
    vis_end = jnp.minimum(s_valid, ((last >> CHUNK_SHIFT) + 1) << CHUNK_SHIFT)
    return (vis_end + tk - 1) // tk


def _select_kernel(iq_ref, iw_ref, ki_ref, mask_ref, skey_ref, thr_ref, jcut_ref, *, tq, tk, n_tiles, past,
                   s_valid, topk):
    i = pl.program_id(1)
    nk = _visible_tiles(i, tq=tq, tk=tk, past=past, s_valid=s_valid)
    qchunk = (past + i * tq + lax.broadcasted_iota(I32, (tq, tk), 0)) >> CHUNK_SHIFT
    lane_k = lax.broadcasted_iota(I32, (tq, tk), 1)
    iw = iw_ref[...]
    wcols = [jnp.broadcast_to(iw[:, h:h + 1], (tq, LANES)) for h in range(IDX_HEADS)]

    def score_body(j, carry):
        kt = ki_ref[pl.ds(j * tk, tk), :]
        acc = [jnp.zeros((tq, LANES), F32) for _ in range(tk // LANES)]
        for h in range(IDX_HEADS):
            lg = lax.dot_general(iq_ref[h], kt, (((1,), (1,)), ((), ())), preferred_element_type=F32)
            for c in range(tk // LANES):
                acc[c] = acc[c] + wcols[h] * jnp.maximum(lg[:, c * LANES:(c + 1) * LANES], 0.0)
        score = jnp.concatenate(acc, axis=1)
        bits = lax.bitcast_convert_type(score, I32)
        key = bits ^ ((bits >> 31) & 0x7FFFFFFF)
        kpos = j * tk + lane_k
        vis = ((kpos >> CHUNK_SHIFT) <= qchunk) & (kpos < s_valid)
        skey_ref[:, pl.ds(j * tk, tk)] = jnp.where(vis, key, INT_MIN)
        return carry

    lax.fori_loop(0, nk, score_body, 0)

    rg = min(tq, 128)
    lane_g = lax.broadcasted_iota(I32, (rg, LANES), 1)

    def count(r0, pred):
        def body(j, cnt):
            t = skey_ref[r0:r0 + rg, pl.ds(j * tk, tk)]
            for c in range(tk // LANES):
                cnt = cnt + jnp.where(pred(t[:, c * LANES:(c + 1) * LANES], j * tk + c * LANES), 1, 0)
            return cnt

        cnt = lax.fori_loop(0, nk, body, jnp.zeros((rg, LANES), I32))
        return jnp.sum(cnt, axis=1, keepdims=True)

    def count_ge(r0, cand):
        cand_b = jnp.broadcast_to(cand, (rg, LANES))
        return count(r0, lambda t, k0: t >= cand_b)

    n_ties = jnp.zeros((1, 1), I32)
    for r0 in range(0, tq, rg):
        c0 = count_ge(r0, jnp.zeros((rg, 1), I32))
        nonneg = c0 >= topk
        prefix = jnp.where(nonneg, 0, INT_MIN).astype(I32)
        n_ge = jnp.where(nonneg, c0, nk * tk)

        def bit_body(it, carry, r0=r0):
            prefix, n_ge = carry
            cand = prefix + lax.shift_left(jnp.int32(1), 30 - it)
            c = count_ge(r0, cand)
            ok = c >= topk
            return jnp.where(ok, cand, prefix), jnp.where(ok, c, n_ge)

        prefix, n_ge = lax.fori_loop(0, 31, bit_body, (prefix, n_ge))
        thr_ref[r0:r0 + rg] = jnp.broadcast_to(jnp.maximum(prefix, INT_MIN + 1), (rg, LANES))
        tied = (n_ge > topk) & (prefix > INT_MIN)
        n_ties = n_ties + jnp.sum(jnp.where(tied, 1, 0), axis=0, keepdims=True)
    jcut_ref[...] = jnp.full(jcut_ref.shape, n_tiles * tk, I32)

    @pl.when(n_ties[0, 0] > 0)
    def _():
        for r0 in range(0, tq, rg):
            thr = thr_ref[r0:r0 + rg]
            need = topk - count(r0, lambda t, k0: t > thr)

            def idx_body(it, jcut, r0=r0, thr=thr, need=need):
                cand = jcut + lax.shift_left(jnp.int32(1), INDEX_BITS - 1 - it)
                cand_b = jnp.broadcast_to(cand, (rg, LANES))
                c = count(r0, lambda t, k0: (t == thr) & (k0 + lane_g < cand_b))
                return jnp.where(c < need, cand, jcut)

            jcut = lax.fori_loop(0, INDEX_BITS, idx_body, jnp.zeros((rg, 1), I32))
            jcut_ref[r0:r0 + rg] = jnp.broadcast_to(jcut, (rg, LANES))

    thr_all, jcut_all = thr_ref[...], jcut_ref[...]
    lane_q = lax.broadcasted_iota(I32, (tq, LANES), 1)

    def mask_body(j, carry):
        t = skey_ref[:, pl.ds(j * tk, tk)]
        cols = []
        for c in range(tk // LANES):
            tc = t[:, c * LANES:(c + 1) * LANES]
            kidx = j * tk + c * LANES + lane_q
            cols.append(jnp.where((tc > thr_all) | ((tc == thr_all) & (kidx <= jcut_all)), 1, 0))
        mask_ref[:, pl.ds(j * tk, tk)] = jnp.concatenate(cols, axis=1).astype(mask_ref.dtype)
        return carry

    lax.fori_loop(0, nk, mask_body, 0)

    def zero_body(j, carry):
        mask_ref[:, pl.ds(j * tk, tk)] = jnp.zeros((tq, tk), mask_ref.dtype)
        return carry

    lax.fori_loop(nk, n_tiles, zero_body, 0)


def _bit_transpose32(rows):
    rows = list(rows)
    j, m = 16, 0x0000FFFF
    while j:
        k = 0
        while k < 32:
            t = (rows[k] ^ lax.shift_right_logical(rows[k + j], j)) & _as_i32(m)
            rows[k] = rows[k] ^ t
            rows[k + j] = rows[k + j] ^ lax.shift_left(t, j)
            k = (k + j + 1) & ~j
        j >>= 1
        m = (m ^ (m << j)) & 0xFFFFFFFF
    return rows


def _as_i32(m):
    return m - (1 << 32) if m >= (1 << 31) else m


def _select_cols_kernel(iq_ref, iwt_ref, ki_ref, mask_ref, skey_ref, jcut_ref, planes_ref, alive_ref, ones_ref, *,
                        tq, tk, n_tiles, past, s_valid, topk):
    i = pl.program_id(1)
    nk = _visible_tiles(i, tq=tq, tk=tk, past=past, s_valid=s_valid)
    sub = 32
    gpt = tk // WORD_KEYS
    qchunk = (past + i * tq + lax.broadcasted_iota(I32, (WORD_KEYS, tq), 1)) >> CHUNK_SHIFT
    row_g = lax.broadcasted_iota(I32, (WORD_KEYS, tq), 0)
    row_k = lax.broadcasted_iota(I32, (tk, tq), 0)
    row_s = lax.broadcasted_iota(I32, (sub, tq), 0)

    def score_body(j, carry, masked):
        for g in range(gpt):
            k0 = j * tk + g * WORD_KEYS
            kt = ki_ref[pl.ds(k0, WORD_KEYS), :]
            acc = jnp.zeros((WORD_KEYS, tq), F32)
            for h in range(IDX_HEADS):
                lg = lax.dot_general(kt, iq_ref[h], (((1,), (1,)), ((), ())), preferred_element_type=F32)
                acc = acc + iwt_ref[h:h + 1, :] * jnp.maximum(lg, 0.0)
            bits = lax.bitcast_convert_type(acc, I32)
            ukey = bits ^ ((bits >> 31) | INT_MIN)
            if masked:
                kpos = k0 + row_g
                vis = ((kpos >> CHUNK_SHIFT) <= qchunk) & (kpos < s_valid)
                ukey = jnp.where(vis, ukey, 0)
            skey_ref[pl.ds(k0, WORD_KEYS), :] = ukey ^ INT_MIN
            slabs = _bit_transpose32([ukey[8 * v:8 * v + 8] for v in range(32)])
            for p in range(32):
                planes_ref[j * gpt + g, p] = slabs[p]
            alive_ref[j * gpt + g] = jnp.full((8, tq), -1, I32)
        return carry

    first_chunk_end = (((past + i * tq) >> CHUNK_SHIFT) + 1) << CHUNK_SHIFT
    n_full = jnp.minimum(jnp.minimum(first_chunk_end // tk, s_valid // tk), nk)
    lax.fori_loop(0, n_full, functools.partial(score_body, masked=False), 0)
    lax.fori_loop(n_full, nk, functools.partial(score_body, masked=True), 0)

    n_pairs = (nk + 1) // 2

    @pl.when(nk < 2 * n_pairs)
    def _():
        for g in range(gpt):
            planes_ref[nk * gpt + g] = jnp.zeros((32, 8, tq), I32)
            alive_ref[nk * gpt + g] = jnp.zeros((8, tq), I32)

    srl = lax.shift_right_logical

    def nibble_counts(x):
        x = x - (srl(x, 1) & 0x55555555)
        return (x & 0x33333333) + (srl(x, 2) & 0x33333333)

    def byte_counts(x):
        return (x & 0x0F0F0F0F) + (srl(x, 4) & 0x0F0F0F0F)

    def byte_sum(x):
        x = (x & 0x00FF00FF) + (srl(x, 8) & 0x00FF00FF)
        return (x + srl(x, 16)) & 0xFFFF

    def popcount(x):
        return byte_sum(byte_counts(nibble_counts(x)))

    def bit_body(p, carry):
        thr_u, need, take_prev = carry
        take_b = jnp.broadcast_to(take_prev, (8, tq)) != 0

        def body(j, cnts):
            nib = []
            for g in range(2 * gpt):
                gi = j * 2 * gpt + g
                prev_ones = ones_ref[gi]
                alive = jnp.where(take_b, prev_ones, alive_ref[gi] ^ prev_ones)
                ones = alive & planes_ref[gi, p]
                alive_ref[gi] = alive
                ones_ref[gi] = ones
                nib.append(nibble_counts(ones))
            return tuple(cnt + byte_counts(nib[2 * a] + nib[2 * a + 1]) for a, cnt in enumerate(cnts))

        cnts = lax.fori_loop(0, n_pairs, body, tuple(jnp.zeros((8, tq), I32) for _ in range(gpt)))
        c = jnp.sum(sum(byte_sum(cnt) for cnt in cnts), axis=0, keepdims=True)
        take = c >= need
        thr_u = thr_u | jnp.where(take, lax.shift_left(jnp.int32(1), 31 - p), 0)
        return thr_u, jnp.where(take, need, need - c), jnp.where(take, 1, 0)

    def clear_body(j, carry):
        for g in range(2 * gpt):
            ones_ref[j * 2 * gpt + g] = jnp.zeros((8, tq), I32)
        return carry

    lax.fori_loop(0, n_pairs, clear_body, 0)
    zero = jnp.zeros((1, tq), I32)
    thr_u, need, take_last = lax.fori_loop(0, 32, bit_body, (zero, zero + topk, zero))
    take_b = jnp.broadcast_to(take_last, (8, tq)) != 0

    def equal_body(j, cnt):
        for g in range(2 * gpt):
            gi = j * 2 * gpt + g
            cnt = cnt + popcount(jnp.where(take_b, ones_ref[gi], alive_ref[gi] ^ ones_ref[gi]))
        return cnt

    n_eq = jnp.sum(lax.fori_loop(0, n_pairs, equal_body, jnp.zeros((8, tq), I32)), axis=0, keepdims=True)
    thr = jnp.maximum(thr_u ^ INT_MIN, INT_MIN + 1)
    tied = (n_eq > need) & (thr_u != 0)
    n_ties = jnp.sum(jnp.where(tied, 1, 0), axis=1, keepdims=True)
    jcut_ref[...] = jnp.full(jcut_ref.shape, n_tiles * tk, I32)

    @pl.when(n_ties[0, 0] > 0)
    def _():
        thr_b = jnp.broadcast_to(thr, (sub, tq))

        def count(pred):
            def body(j, cnt):
                for r in range(tk // sub):
                    k0 = j * tk + r * sub
                    cnt = cnt + jnp.where(pred(skey_ref[pl.ds(k0, sub), :], k0), 1, 0)
                return cnt

            cnt = lax.fori_loop(0, nk, body, jnp.zeros((sub, tq), I32))
            return jnp.sum(cnt, axis=0, keepdims=True)

        def idx_body(it, jcut):
            cand_b = jnp.broadcast_to(jcut + lax.shift_left(jnp.int32(1), INDEX_BITS - 1 - it), (sub, tq))
            c = count(lambda t, k0: (t == thr_b) & (k0 + row_s < cand_b))
            return jnp.where(c < need, cand_b[:1], jcut)

        jcut = lax.fori_loop(0, INDEX_BITS, idx_body, jnp.zeros((1, tq), I32))
        jcut_ref[...] = jnp.broadcast_to(jcut, jcut_ref.shape)

    thr_t = jnp.broadcast_to(thr, (tk, tq))
    jcut_t = jnp.broadcast_to(jcut_ref[:1], (tk, tq))

    def mask_body(j, carry):
        t = skey_ref[pl.ds(j * tk, tk), :]
        keep = (t > thr_t) | ((t == thr_t) & (j * tk + row_k <= jcut_t))
        mask_ref[:, pl.ds(j * tk, tk)] = jnp.where(keep, 1, 0).T.astype(mask_ref.dtype)
        return carry

    lax.fori_loop(0, nk, mask_body, 0)

    def zero_body(j, carry):
        mask_ref[:, pl.ds(j * tk, tk)] = jnp.zeros((tq, tk), mask_ref.dtype)
        return carry

    lax.fori_loop(nk, n_tiles, zero_body, 0)


def _dsa_select(iq, iw, ki_all, *, b, t, past, s_valid, tk):
    s_pad = ki_all.shape[1]
    tq = _row_tile(t, 256)
    nq = t // tq
    topk = min(TOPK_MAX, s_valid // 4)
    assert s_pad < 2 ** INDEX_BITS
    if tq % LANES == 0:
        assert tk % WORD_KEYS == 0 and (s_pad // tk) % 2 == 0
        assert (s_pad // tk) // 2 * 16 <= 255
        kern = functools.partial(_select_cols_kernel, tq=tq, tk=tk, n_tiles=s_pad // tk, past=past,
                                 s_valid=s_valid, topk=topk)
        iwt = iw[:, :IDX_HEADS].reshape(b, t, IDX_HEADS).transpose(0, 2, 1)
        return pl.pallas_call(
            kern,
            grid=(b, nq),
            in_specs=[pl.BlockSpec((IDX_HEADS, tq, LANES), lambda bi, i: (0, bi * nq + i, 0)),
                      pl.BlockSpec((None, IDX_HEADS, tq), lambda bi, i: (bi, 0, i)),
                      pl.BlockSpec((None, s_pad, LANES), lambda bi, i: (bi, 0, 0))],
            out_specs=pl.BlockSpec((None, tq, s_pad), lambda bi, i: (bi, i, 0)),
            out_shape=jax.ShapeDtypeStruct((b, t, s_pad), MASK_DTYPE),
            scratch_shapes=[pltpu.VMEM((s_pad, tq), I32), pltpu.VMEM((8, tq), I32),
                            pltpu.VMEM((s_pad // WORD_KEYS, 32, 8, tq), I32),
                            pltpu.VMEM((s_pad // WORD_KEYS, 8, tq), I32),
                            pltpu.VMEM((s_pad // WORD_KEYS, 8, tq), I32)],
            compiler_params=_cparams(("parallel", "parallel")),
            name="dsa_select",
        )(iq, iwt, ki_all)
    kern = functools.partial(_select_kernel, tq=tq, tk=tk, n_tiles=s_pad // tk, past=past, s_valid=s_valid,
                             topk=topk)
    return pl.pallas_call(
        kern,
        grid=(b, nq),
        in_specs=[pl.BlockSpec((IDX_HEADS, tq, LANES), lambda bi, i: (0, bi * nq + i, 0)),
                  pl.BlockSpec((tq, LANES), lambda bi, i: (bi * nq + i, 0)),
                  pl.BlockSpec((None, s_pad, LANES), lambda bi, i: (bi, 0, 0))],
        out_specs=pl.BlockSpec((None, tq, s_pad), lambda bi, i: (bi, i, 0)),
        out_shape=jax.ShapeDtypeStruct((b, t, s_pad), MASK_DTYPE),
        scratch_shapes=[pltpu.VMEM((tq, s_pad), I32), pltpu.VMEM((tq, LANES), I32), pltpu.VMEM((tq, LANES), I32)],
        compiler_params=_cparams(("parallel", "parallel")),
        name="dsa_select",
    )(iq, iw, ki_all)


def _attn_kernel(*refs, tq, tk, hpg, dqk, dv, past, s_valid, scale, has_mask, kv_shared):
    if has_mask:
        q_ref, k_ref, v_ref, mask_ref, o_ref, m_ref, acc_ref, sa_ref, sb_ref = refs
    else:
        q_ref, k_ref, v_ref, o_ref, m_ref, acc_ref, sa_ref, sb_ref, vis_ref = refs
    i = pl.program_id(2)
    nk = _visible_tiles(i, tq=tq, tk=tk, past=past, s_valid=s_valid)
    m_ref[...] = jnp.full(m_ref.shape, NEG_INF, F32)
    acc_ref[...] = jnp.zeros(acc_ref.shape, F32)
    lane_k = lax.broadcasted_iota(I32, (tq, tk), 1)
    if not has_mask:
        qchunk = (past + i * tq + lax.broadcasted_iota(I32, (tq, tk), 0)) >> CHUNK_SHIFT
        vis_ref[...] = qchunk - (lane_k >> CHUNK_SHIFT)
    c = scale * LOG2_E
    ones_col = jnp.where(lax.broadcasted_iota(I32, (tk, LANES), 1) == 0, 1.0, 0.0).astype(BF16)

    def scores(j, s_ref):
        rows = pl.ds(j * tk, tk)
        for h in range(hpg):
            hk = 0 if kv_shared else h
            kt = k_ref[rows, hk * dqk:(hk + 1) * dqk]
            q = q_ref[:, h * dqk:(h + 1) * dqk]
            s = lax.dot_general(q, kt, (((1,), (1,)), ((), ())), preferred_element_type=F32)
            s_ref[h] = s * c

    def softmax_pv(j, s_ref, masked=True):
        rows = pl.ds(j * tk, tk)
        if not masked:
            keep = None
        elif has_mask:
            keep = mask_ref[:, rows].astype(I32) != 0
        else:
            keep = vis_ref[...] >= ((j * tk) >> CHUNK_SHIFT)
            if s_valid % tk:
                keep = keep & (j * tk + lane_k < s_valid)
        for h in range(hpg):
            hk = 0 if kv_shared else h
            v1 = jnp.concatenate([v_ref[rows, hk * dv:(hk + 1) * dv], ones_col], axis=1)
            s = jnp.where(keep, s_ref[h], NEG_INF) if masked else s_ref[h]
            m_prev = m_ref[h]
            m_new = jnp.maximum(m_prev, jnp.max(s, axis=1, keepdims=True))
            alpha = jnp.exp2(m_prev - m_new)
            p = jnp.exp2(s - jnp.tile(m_new, (1, tk // LANES)))
            pv = jnp.dot(p.astype(BF16), v1, preferred_element_type=F32)
            acc_ref[h] = jnp.tile(alpha, (1, 2)) * acc_ref[h] + pv
            m_ref[h] = m_new

    def pair_body(jj, carry, masked):
        j0 = 2 * jj
        scores(j0 + 1, sb_ref)
        softmax_pv(j0, sa_ref, masked)
        scores(j0 + 2, sa_ref)
        softmax_pv(j0 + 1, sb_ref, masked)
        return carry

    scores(0, sa_ref)
    n_pairs = (nk - 1) // 2
    if has_mask:
        n_free = 0
    else:
        first_chunk_end = (((past + i * tq) >> CHUNK_SHIFT) + 1) << CHUNK_SHIFT
        n_free = jnp.minimum(jnp.minimum(first_chunk_end, s_valid) // (2 * tk), n_pairs)
        lax.fori_loop(0, n_free, functools.partial(pair_body, masked=False), 0)
    lax.fori_loop(n_free, n_pairs, functools.partial(pair_body, masked=True), 0)
    j0 = 2 * n_pairs

    @pl.when(nk - j0 == 2)
    def _():
        scores(j0 + 1, sb_ref)
        softmax_pv(j0, sa_ref)
        softmax_pv(j0 + 1, sb_ref)

    @pl.when(nk - j0 == 1)
    def _():
        softmax_pv(j0, sa_ref)

    for h in range(hpg):
        acc = acc_ref[h]
        o_ref[:, h * dv:(h + 1) * dv] = (acc[:, :dv] / acc[:, dv:dv + 1]).astype(o_ref.dtype)


def _attention(q, k, v, mask, *, groups, hpg, dqk, kv_shared, past, s_valid, tk, scale):
    b, t, _ = q.shape
    s_pad = k.shape[1]
    dv = LANES
    kvh = 1 if kv_shared else hpg
    tq = _row_tile(t, 1024)
    kern = functools.partial(_attn_kernel, tq=tq, tk=tk, hpg=hpg, dqk=dqk, dv=dv, past=past, s_valid=s_valid,
                             scale=scale, has_mask=mask is not None, kv_shared=kv_shared)
    in_specs = [pl.BlockSpec((None, tq, hpg * dqk), lambda bi, g, i: (bi, i, g)),
                pl.BlockSpec((None, s_pad, kvh * dqk), lambda bi, g, i: (bi, 0, g)),
                pl.BlockSpec((None, s_pad, kvh * dv), lambda bi, g, i: (bi, 0, g))]
    args = [q, k, v]
    if mask is not None:
        in_specs.append(pl.BlockSpec((None, tq, s_pad), lambda bi, g, i: (bi, i, 0)))
        args.append(mask)
    return pl.pallas_call(
        kern,
        grid=(b, groups, t // tq),
        in_specs=in_specs,
        out_specs=pl.BlockSpec((None, tq, hpg * dv), lambda bi, g, i: (bi, i, g)),
        out_shape=jax.ShapeDtypeStruct((b, t, groups * hpg * dv), BF16),
        scratch_shapes=[pltpu.VMEM((hpg, tq, LANES), F32), pltpu.VMEM((hpg, tq, 2 * dv), F32),
                        pltpu.VMEM((hpg, tq, tk), F32), pltpu.VMEM((hpg, tq, tk), F32)]
        + ([] if mask is not None else [pltpu.VMEM((tq, tk), I32)]),
        compiler_params=_cparams(("parallel", "parallel", "arbitrary")),
        name="dsa_attention" if mask is not None else "mla_attention",
    )(*args)


def _merge_kernel(x_ref, oa_ref, ob_ref, oc_ref, g0_ref, g1_ref, g2_ref, bg_ref, wa_ref, wb_ref, wc_ref, wo_ref,
                  o_ref):
    merged = None
    for n, (o_r, g_r, w_r) in enumerate(((oa_ref, g0_ref, wa_ref), (ob_ref, g1_ref, wb_ref),
                                         (oc_ref, g2_ref, wc_ref))):
        gate = _sigmoid(g_r[...] + bg_ref[:, n * D_MODEL:(n + 1) * D_MODEL])
        term = gate * jnp.dot(o_r[...], w_r[...], preferred_element_type=F32)
        merged = term if merged is None else merged + term
    o_ref[...] = x_ref[...] + jnp.dot(merged.astype(BF16), wo_ref[...], preferred_element_type=F32)


def _merge(x, oa, ob, oc, z, bg, wa, wb, wc, wo):
    m, d = x.shape
    tm = _row_tile(m, 512)
    rows = pl.BlockSpec((tm, d), lambda i: (i, 0))
    gblk = lambda n: pl.BlockSpec((tm, d), lambda i: (i, Z_G // d + n))
    full = lambda shape: pl.BlockSpec(shape, lambda i: (0,) * len(shape))
    return pl.pallas_call(
        _merge_kernel,
        grid=(m // tm,),
        in_specs=[rows, rows, rows, rows, gblk(0), gblk(1), gblk(2), full((1, 3 * d)),
                  full((d, d)), full((d, d)), full((d, d)), full((d, d))],
        out_specs=rows,
        out_shape=jax.ShapeDtypeStruct((m, d), F32),
        compiler_params=_cparams(("parallel",)),
        name="merge",
    )(x, oa, ob, oc, z, z, z, bg.reshape(1, -1), wa, wb, wc, wo)


def _ffn_kernel(*refs, final):
    if final:
        x_ref, g_ref, wg_ref, wu_ref, wd_ref, gf_ref, o_ref, hn_ref, acc_ref = refs
    else:
        x_ref, g_ref, wg_ref, wu_ref, wd_ref, o_ref, hn_ref, acc_ref = refs
    j = pl.program_id(1)

    @pl.when(j == 0)
    def _():
        hn_ref[...] = _rms(x_ref[...], g_ref[...]).astype(BF16)
        acc_ref[...] = jnp.zeros(acc_ref.shape, F32)

    hn = hn_ref[...]
    gt = jnp.dot(hn, wg_ref[...], preferred_element_type=F32)
    up = jnp.dot(hn, wu_ref[...], preferred_element_type=F32)
    act = (gt * _sigmoid(gt) * up).astype(BF16)
    acc_ref[...] += jnp.dot(act, wd_ref[...], preferred_element_type=F32)

    @pl.when(j == pl.num_programs(1) - 1)
    def _():
        y = x_ref[...] + acc_ref[...]
        o_ref[...] = _rms(y, gf_ref[...]) if final else y


def _ffn(x, g, wg, wu, wd, gf=None):
    m, d = x.shape
    f = wg.shape[1]
    tm, tf = _row_tile(m, 1024), f // 2
    final = gf is not None
    in_specs = [pl.BlockSpec((tm, d), lambda i, j: (i, 0)), pl.BlockSpec((1, d), lambda i, j: (0, 0)),
                pl.BlockSpec((d, tf), lambda i, j: (0, j)), pl.BlockSpec((d, tf), lambda i, j: (0, j)),
                pl.BlockSpec((tf, d), lambda i, j: (j, 0))]
    args = [x, g.reshape(1, d), wg, wu, wd]
    if final:
        in_specs.append(pl.BlockSpec((1, d), lambda i, j: (0, 0)))
        args.append(gf.reshape(1, d))
    return pl.pallas_call(
        functools.partial(_ffn_kernel, final=final),
        grid=(m // tm, f // tf),
        in_specs=in_specs,
        out_specs=pl.BlockSpec((tm, d), lambda i, j: (i, 0)),
        out_shape=jax.ShapeDtypeStruct((m, d), F32),
        scratch_shapes=[pltpu.VMEM((tm, d), BF16), pltpu.VMEM((tm, d), F32)],
        compiler_params=_cparams(("parallel", "arbitrary")),
        name="ffn",
    )(*args)


def _pad_cols(w, width):
    return jnp.concatenate([w, jnp.zeros(w.shape[:-1] + (width - w.shape[-1],), w.dtype)], axis=-1)


def _layout_w_in(w):
    a_gate, a_x, b_q, b_kv, b_kr = w[:, 0:1024], w[:, 1024:2048], w[:, 2048:2432], w[:, 2432:2688], w[:, 2688:2752]
    c_q, c_k, c_v, c_iq = w[:, 2752:3776], w[:, 3776:4288], w[:, 4288:4800], w[:, 4800:5824]
    c_ik, c_iw, g = w[:, 5824:5888], w[:, 5888:5904], w[:, 5904:8976]
    out = jnp.concatenate([a_gate, a_x, c_q, c_iq, g, c_k, c_v, b_kv, _pad_cols(b_kr, LANES),
                           _pad_cols(c_ik, LANES), _pad_cols(c_iw, LANES), b_q], axis=1)
    assert out.shape[1] == Z_WIDTH
    return out.astype(BF16)


def _layout_mla(w_q_up, w_kv_up):
    r = w_q_up.shape[0]
    zeros = jnp.zeros((r, MLA_HEADS, MLA_QK_PAD - MLA_NOPE - MLA_ROPE), w_q_up.dtype)
    wq = jnp.concatenate([w_q_up, zeros], axis=-1).reshape(r, MLA_HEADS * MLA_QK_PAD).astype(BF16)
    wk = w_kv_up[:, :, :MLA_NOPE].reshape(MLA_KV_LORA, MLA_HEADS * MLA_NOPE).astype(BF16)
    wv = w_kv_up[:, :, MLA_NOPE:].reshape(MLA_KV_LORA, MLA_HEADS * MLA_V).astype(BF16)
    return wq, wk, wv


def _key_tile(s_valid):
    return 512 if s_valid % 512 == 0 else 384


def _pad_keys(x, s_pad):
    b, s = x.shape[:2]
    if s == s_pad:
        return x
    return jnp.concatenate([x, jnp.zeros((b, s_pad - s) + x.shape[2:], x.dtype)], axis=1)


def _layer(x3, pos, past, w, final_g):
    (attn_norm_g, w_in, b_gates, lru_conv_w, lru_conv_b, lru_wa, lru_ba, lru_wx, lru_bx, lru_lambda,
     mla_q_norm_g, mla_kv_norm_g, mla_w_q_up, mla_w_kv_up, idx_k_norm_g, idx_k_norm_b,
     w_branch_a, w_branch_b, w_branch_c, w_out, ffn_norm_g, w_ffn_gate, w_ffn_up, w_ffn_down) = w
    past_lat, past_kr, past_k, past_v, past_ki, conv_buf, h0 = past
    b, t, d = x3.shape
    m = b * t
    past_len = 0 if past_lat is None else past_lat.shape[1]
    s_valid = past_len + t
    tk = _key_tile(s_valid)
    s_pad = -(-s_valid // tk) * tk
    x = x3.reshape(m, d)

    z = _norm_matmul(x, attn_norm_g, _layout_w_in(w_in))
    z3 = z.reshape(b, t, Z_WIDTH)

    o_a, conv8, h8 = _lru(z3, conv_buf, h0, lru_conv_w, lru_conv_b, lru_wa, lru_ba, lru_wx, lru_bx,
                          lru_lambda.reshape(-1))
    conv_new, h_new = conv8[:, 8 - (CONV_W - 1):], h8[:, 7]

    wq, wk, wv = _layout_mla(mla_w_q_up, mla_w_kv_up)
    cos64, sin64 = _rope_tables(pos, MLA_ROPE, MLA_ROPE)
    q_b, lat_new, kr_pad = _mla_q(z, cos64, sin64, mla_q_norm_g, mla_kv_norm_g, wq)
    lat_new = lat_new.reshape(b, t, MLA_KV_LORA)
    kr_pad = kr_pad.reshape(b, t, LANES)
    kr_new = kr_pad[:, :, :MLA_ROPE]
    if past_lat is None:
        lat_all, kr_all = lat_new, kr_pad
    else:
        lat_all = jnp.concatenate([past_lat.astype(F32), lat_new], axis=1)
        kr_all = jnp.concatenate([_pad_cols(past_kr.astype(F32), LANES), kr_pad], axis=1)
    lat_all, kr_all = _pad_keys(lat_all, s_pad), _pad_keys(kr_all, s_pad)
    k_b, v_b = _mla_kv(lat_all.reshape(b * s_pad, -1), kr_all.reshape(b * s_pad, -1), wk, wv)
    o_b = _attention(q_b.reshape(b, t, -1), k_b.reshape(b, s_pad, -1), v_b.reshape(b, s_pad, -1), None,
                     groups=MLA_HEADS // 2, hpg=2, dqk=MLA_QK_PAD, kv_shared=False, past=past_len,
                     s_valid=s_valid, tk=tk, scale=(MLA_NOPE + MLA_ROPE) ** -0.5)

    tabs = _rope_tables(pos, DSA_HD, DSA_HD) + _rope_tables(pos, IDX_ROPE, IDX_HD)
    q_c, k_new, iq, ki_pad, iw = _dsa_prep(z, tabs, idx_k_norm_g, idx_k_norm_b)
    v_new = z3[:, :, Z_C_V:Z_C_V + DSA_KV_HEADS * DSA_HD]
    k_new = k_new.reshape(b, t, -1)
    ki_pad = ki_pad.reshape(b, t, LANES)
    ki_new = ki_pad[:, :, :IDX_HD]
    if past_k is None:
        k_all, v_all, ki_all = k_new, v_new, ki_pad
    else:
        k_all = jnp.concatenate([past_k.astype(F32).reshape(b, past_len, -1), k_new], axis=1)
        v_all = jnp.concatenate([past_v.astype(F32).reshape(b, past_len, -1), v_new], axis=1)
        ki_all = jnp.concatenate([_pad_cols(past_ki.astype(F32), LANES), ki_pad], axis=1)
    k_all, v_all, ki_all = (_pad_keys(a.astype(BF16), s_pad) for a in (k_all, v_all, ki_all))
    keep = _dsa_select(iq, iw, ki_all, b=b, t=t, past=past_len, s_valid=s_valid, tk=tk)
    o_c = _attention(q_c.reshape(b, t, -1), k_all, v_all, keep, groups=DSA_KV_HEADS,
                     hpg=DSA_HEADS // DSA_KV_HEADS, dqk=DSA_HD, kv_shared=True, past=past_len, s_valid=s_valid,
                     tk=tk, scale=DSA_HD ** -0.5)

    bf = lambda a: a.astype(BF16)
    x = _merge(x, o_a.reshape(m, -1), o_b.reshape(m, -1), o_c.reshape(m, -1), z, b_gates,
               bf(w_branch_a), bf(w_branch_b), bf(w_branch_c), bf(w_out))
    x = _ffn(x, ffn_norm_g, bf(w_ffn_gate), bf(w_ffn_up), bf(w_ffn_down), final_g)
    new = (lat_new, kr_new, k_new.reshape(b, t, DSA_KV_HEADS, DSA_HD), v_new.reshape(b, t, DSA_KV_HEADS, DSA_HD),
           ki_new, conv_new, h_new)
    return x.reshape(b, t, d), new


def _trunk(x, caches, weights, final_norm_g):
    b, t, _ = x.shape
    depth = weights[0].shape[0]
    past_len = 0 if caches is None else caches[0].shape[2]
    pos = past_len + jnp.arange(t, dtype=I32)
    new = []
    for l in range(depth):
        w_l = tuple(wt[l] for wt in weights)
        if caches is None:
            past = (None, None, None, None, None, jnp.zeros((b, CONV_W - 1, LRU_WIDTH), F32),
                    jnp.zeros((b, LRU_WIDTH), F32))
        else:
            past = tuple(c[l] for c in caches)
        x, st = _layer(x, pos, past, w_l, final_norm_g if l == depth - 1 else None)
        new.append(st)
    return x, tuple(jnp.stack([s[i] for s in new]) for i in range(7))


def kernel(x_prompt, x_sample, cache_mla_latent, cache_mla_krope, cache_dsa_k, cache_dsa_v, cache_dsa_kidx,
           state_lru_conv, state_lru_h, attn_norm_g, w_in, b_gates, lru_conv_w, lru_conv_b, lru_wa, lru_ba,
           lru_wx, lru_bx, lru_lambda, mla_q_norm_g, mla_kv_norm_g, mla_w_q_up, mla_w_kv_up, idx_k_norm_g,
           idx_k_norm_b, w_branch_a, w_branch_b, w_branch_c, w_out, ffn_norm_g, w_ffn_gate, w_ffn_up,
           w_ffn_down, final_norm_g):
    weights = (attn_norm_g, w_in, b_gates, lru_conv_w, lru_conv_b, lru_wa, lru_ba, lru_wx, lru_bx, lru_lambda,
               mla_q_norm_g, mla_kv_norm_g, mla_w_q_up, mla_w_kv_up, idx_k_norm_g, idx_k_norm_b,
               w_branch_a, w_branch_b, w_branch_c, w_out, ffn_norm_g, w_ffn_gate, w_ffn_up, w_ffn_down)
    y_p, st_p = _trunk(x_prompt, None, weights, final_norm_g)
    caches = (cache_mla_latent, cache_mla_krope, cache_dsa_k, cache_dsa_v, cache_dsa_kidx, state_lru_conv,
              state_lru_h)
    y_s, st_s = _trunk(x_sample, caches, weights, final_norm_g)
    (lat_p, kr_p, k_p, v_p, ki_p, conv_p, h_p) = st_p
    (lat_s, kr_s, k_s, v_s, ki_s, conv_s, h_s) = st_s
    return (y_p, y_s, lat_p, lat_s, kr_p, kr_s, k_p, k_s, v_p, v_s, ki_p, ki_s, conv_p, conv_s, h_p, h_s)
```
